```python
import math
import jax, jax.numpy as jnp
from jax import lax
import numpy as np

D_MODEL = 2048
BATCH = 1
SEQ = 16384
DEPTH = 1
DEC_BATCH = 32
DEC_SEQ = 4
PAST_LEN = 16384
PAGE_SIZE = 128

MIX_WIDTH = D_MODEL
HEAD_DIM = D_MODEL // 16
DIFF_WIDTH = MIX_WIDTH // 2
N_HEADS = DIFF_WIDTH // (2 * HEAD_DIM)
POOL_WIDTH = MIX_WIDTH // 4
POOL_WINDOWS = (2, 4, 8, 16)
POOL_GROUPS = len(POOL_WINDOWS)
POOL_GROUP_DIM = POOL_WIDTH // POOL_GROUPS
POOL_HIST = max(POOL_WINDOWS) - 1
MEM_LEN = 256
MEM_HEADS = 4
MEM_WIDTH = MIX_WIDTH // 4
MEM_HEAD_DIM = MEM_WIDTH // MEM_HEADS
IN_WIDTH = 3 * DIFF_WIDTH + POOL_WIDTH + MEM_WIDTH
D_FF = ((8 * D_MODEL // 3 + 255) // 256) * 256
CONV_W = 3
ROPE_THETA = 10000.0
Q_BLOCK = 128
NORM_EPS = 1e-6

kernel_name = "hymba_diffattn_multipool_memxattn_convffn_step"


def _rms(x, g):
    xf = x.astype(jnp.float32)
    r = lax.rsqrt(jnp.mean(xf * xf, axis=-1, keepdims=True) + NORM_EPS)
    return (xf * r * g.astype(jnp.float32)).astype(x.dtype)


def _rope(x, pos):
    half = HEAD_DIM // 2
    inv = ROPE_THETA ** (-jnp.arange(half, dtype=jnp.float32) / half)
    ang = pos.astype(jnp.float32)[:, None] * inv[None, :]
    cos = jnp.cos(ang)[:, None, None, :]
    sin = jnp.sin(ang)[:, None, None, :]
    xf = x.astype(jnp.float32)
    x1, x2 = xf[..., :half], xf[..., half:]
    return jnp.concatenate([x1 * cos - x2 * sin, x2 * cos + x1 * sin], axis=-1).astype(x.dtype)


def _split_in(z):
    B, T, _ = z.shape
    q = z[..., :DIFF_WIDTH].reshape(B, T, N_HEADS, 2, HEAD_DIM)
    k = z[..., DIFF_WIDTH:2 * DIFF_WIDTH].reshape(B, T, N_HEADS, 2, HEAD_DIM)
    v = z[..., 2 * DIFF_WIDTH:3 * DIFF_WIDTH].reshape(B, T, N_HEADS, 2 * HEAD_DIM)
    p = z[..., 3 * DIFF_WIDTH:3 * DIFF_WIDTH + POOL_WIDTH]
    cq = z[..., 3 * DIFF_WIDTH + POOL_WIDTH:].reshape(B, T, MEM_HEADS, MEM_HEAD_DIM)
    return q, k, v, p, cq


def _diff_attend(q, k, v, q_pos, k_pos, lam):
    s = jnp.einsum('bqhmd,bkhmd->bhmqk', q, k).astype(jnp.float32) * (HEAD_DIM ** -0.5)
    mask = k_pos[None, :] <= q_pos[:, None]
    p = jax.nn.softmax(jnp.where(mask, s, -jnp.inf), axis=-1)
    a = p[:, :, 0] - lam * p[:, :, 1]
    return jnp.einsum('bhqk,bkhe->bqhe', a.astype(v.dtype), v)


def _attend_blocked(q, k, v, pos, lam):
    B, T = q.shape[0], q.shape[1]
    nb = T // Q_BLOCK
    qb = jnp.moveaxis(q.reshape(B, nb, Q_BLOCK, N_HEADS, 2, HEAD_DIM), 1, 0)
    pb = pos.reshape(nb, Q_BLOCK)
    ob = lax.map(lambda a: _diff_attend(a[0], k, v, a[1], pos, lam), (qb, pb))
    return jnp.moveaxis(ob, 0, 1).reshape(B, T, N_HEADS, 2 * HEAD_DIM)


def _pool_mix(p_ext, pos, w_pool, pool_scale):
    B = p_ext.shape[0]
    T = pos.shape[0]
    c = jnp.cumsum(p_ext.astype(jnp.float32), axis=1)
    c = jnp.pad(c, ((0, 0), (1, 0), (0, 0)))
    end = c[:, POOL_HIST + 1:]
    x_new = p_ext[:, POOL_HIST:].astype(jnp.float32)
    outs = []
    for g, w in enumerate(POOL_WINDOWS):
        sl = slice(g * POOL_GROUP_DIM, (g + 1) * POOL_GROUP_DIM)
        start = c[:, POOL_HIST + 1 - w:POOL_HIST + 1 - w + T, sl]
        cnt = jnp.minimum(pos + 1, w).astype(jnp.float32)[None, :, None]
        outs.append((end[..., sl] - start) / cnt - x_new[..., sl])
    d = jnp.stack(outs, axis=2).astype(p_ext.dtype)
    y = jnp.einsum('btgc,gcd->btgd', d, w_pool).reshape(B, T, POOL_WIDTH)
    return y * pool_scale


def _mem_kv(mem, mem_norm, w_mem_kv):
    B = mem.shape[0]
    z = _rms(mem, mem_norm) @ w_mem_kv
    mk = z[..., :MEM_WIDTH].reshape(B, MEM_LEN, MEM_HEADS, MEM_HEAD_DIM)
    mv = z[..., MEM_WIDTH:].reshape(B, MEM_LEN, MEM_HEADS, MEM_HEAD_DIM)
    return mk, mv


def _mem_attend(cq, mk, mv):
    s = jnp.einsum('bqhd,bkhd->bhqk', cq, mk).astype(jnp.float32) * (MEM_HEAD_DIM ** -0.5)
    p = jax.nn.softmax(s, axis=-1)
    return jnp.einsum('bhqk,bkhd->bqhd', p.astype(mv.dtype), mv)


def _conv_ffn(hn, conv_hist, w_up, conv_w, conv_b, w_down):
    T = hn.shape[1]
    u = hn @ w_up
    u_ext = jnp.concatenate([conv_hist.astype(u.dtype), u], axis=1)
    c = conv_b + conv_w[0] * u_ext[:, 0:T]
    for j in range(1, CONV_W):
        c = c + conv_w[j] * u_ext[:, j:j + T]
    out = (jax.nn.silu(c[..., :D_FF]) * c[..., D_FF:]) @ w_down
    return out, u_ext[:, -(CONV_W - 1):]


def _layer(h, pos, past_k, past_v, pool_hist, conv_hist, mk, mv, lam, lam_init,
           attn_norm, w_in, subln_norm, w_pool, pool_scale, w_o,
           ffn_norm, w_up, conv_w, conv_b, w_down):
    B, T, _ = h.shape
    z = _rms(h, attn_norm) @ w_in
    q, k, v, p_in, cq = _split_in(z)
    q = _rope(q, pos)
    k = _rope(k, pos)
    if past_k is None:
        o_att = _attend_blocked(q, k, v, pos, lam)
    else:
        k_all = jnp.concatenate([past_k.astype(k.dtype), k], axis=1)
        v_all = jnp.concatenate([past_v.astype(v.dtype), v], axis=1)
        k_pos = jnp.arange(k_all.shape[1], dtype=jnp.int32)
        o_att = _diff_attend(q, k_all, v_all, pos, k_pos, lam)
    o_att = (_rms(o_att, subln_norm) * (1.0 - lam_init)).reshape(B, T, DIFF_WIDTH)
    p_ext = jnp.concatenate([pool_hist.astype(p_in.dtype), p_in], axis=1)
    o_pool = _pool_mix(p_ext, pos, w_pool, pool_scale)
    o_mem = _mem_attend(cq, mk, mv).reshape(B, T, MEM_WIDTH)
    h = h + jnp.concatenate([o_att, o_pool, o_mem], axis=-1) @ w_o
    f, conv_tail = _conv_ffn(_rms(h, ffn_norm), conv_hist, w_up, conv_w, conv_b, w_down)
    return h + f, k, v, p_ext[:, -POOL_HIST:], conv_tail


def setup_inputs(seed: int = 0) -> dict:
    key = jax.random.key(seed)
    ks = iter(jax.random.split(key, 40))
    f32 = jnp.float32
    n_pages = PAST_LEN // PAGE_SIZE
    n_used = DEC_BATCH * n_pages
    n_phys = n_used + n_used // 4

    def nrm(shape, scale):
        return jax.random.normal(next(ks), shape, f32) * scale

    def gain(shape):
        return 1.0 + nrm(shape, 0.02)

    page_table = jax.random.permutation(next(ks), n_phys)[:n_used].reshape(DEC_BATCH, n_pages).astype(jnp.int32)
    return {
        "x_prompt": nrm((BATCH, SEQ, D_MODEL), 1.0),
        "x_sample": nrm((DEC_BATCH, DEC_SEQ, D_MODEL), 1.0),
        "cache_k": nrm((DEPTH, n_phys, PAGE_SIZE, N_HEADS, 2, HEAD_DIM), 1.0),
        "cache_v": nrm((DEPTH, n_phys, PAGE_SIZE, N_HEADS, 2 * HEAD_DIM), 1.0),
        "page_table": page_table,
        "state_pool": nrm((DEPTH, DEC_BATCH, POOL_HIST, POOL_WIDTH), 1.0),
        "state_conv": nrm((DEPTH, DEC_BATCH, CONV_W - 1, 2 * D_FF), 1.0),
        "cache_mem_k": nrm((DEPTH, DEC_BATCH, MEM_LEN, MEM_HEADS, MEM_HEAD_DIM), 1.0),
        "cache_mem_v": nrm((DEPTH, DEC_BATCH, MEM_LEN, MEM_HEADS, MEM_HEAD_DIM), 1.0),
        "mem_prompt": nrm((BATCH, MEM_LEN, D_MODEL), 1.0),
        "attn_norm": gain((DEPTH, D_MODEL)),
        "w_in": nrm((DEPTH, D_MODEL, IN_WIDTH), D_MODEL ** -0.5),
        "lam_q1": nrm((DEPTH, HEAD_DIM), 0.1),
        "lam_k1": nrm((DEPTH, HEAD_DIM), 0.1),
        "lam_q2": nrm((DEPTH, HEAD_DIM), 0.1),
        "lam_k2": nrm((DEPTH, HEAD_DIM), 0.1),
        "subln_norm": gain((DEPTH, 2 * HEAD_DIM)),
        "w_pool": nrm((DEPTH, POOL_GROUPS, POOL_GROUP_DIM, POOL_GROUP_DIM), POOL_GROUP_DIM ** -0.5),
        "pool_scale": gain((DEPTH, POOL_WIDTH)),
        "mem_norm": gain((DEPTH, D_MODEL)),
        "w_mem_kv": nrm((DEPTH, D_MODEL, 2 * MEM_WIDTH), D_MODEL ** -0.5),
        "w_o": nrm((DEPTH, MIX_WIDTH, D_MODEL), MIX_WIDTH ** -0.5),
        "ffn_norm": gain((DEPTH, D_MODEL)),
        "w_up": nrm((DEPTH, D_MODEL, 2 * D_FF), D_MODEL ** -0.5),
        "conv_w": nrm((DEPTH, CONV_W, 2 * D_FF), CONV_W ** -0.5),
        "conv_b": nrm((DEPTH, 2 * D_FF), 0.01),
        "w_down": nrm((DEPTH, D_FF, D_MODEL), D_FF ** -0.5),
        "final_norm": gain((D_MODEL,)),
    }


def reference(x_prompt, x_sample, cache_k, cache_v, page_table, state_pool, state_conv,
              cache_mem_k, cache_mem_v, mem_prompt, attn_norm, w_in, lam_q1, lam_k1,
              lam_q2, lam_k2, subln_norm, w_pool, pool_scale, mem_norm, w_mem_kv, w_o,
              ffn_norm, w_up, conv_w, conv_b, w_down, final_norm):
    B, S, _ = x_prompt.shape
    DB, T, _ = x_sample.shape
    past = page_table.shape[1] * cache_k.shape[2]
    pos_p = jnp.arange(S, dtype=jnp.int32)
    pos_s = past + jnp.arange(T, dtype=jnp.int32)
    hp, hs = x_prompt, x_sample
    kp_l, vp_l, ks_l, vs_l, plp_l, pls_l, cvp_l, cvs_l, mkp_l, mvp_l = ([] for _ in range(10))
    for l in range(DEPTH):
        lam_init = 0.8 - 0.6 * math.exp(-0.3 * l)
        lam = (jnp.exp(jnp.sum(lam_q1[l].astype(jnp.float32) * lam_k1[l].astype(jnp.float32)))
               - jnp.exp(jnp.sum(lam_q2[l].astype(jnp.float32) * lam_k2[l].astype(jnp.float32)))
               + lam_init)
        shared = (attn_norm[l], w_in[l], subln_norm[l], w_pool[l], pool_scale[l], w_o[l],
                  ffn_norm[l], w_up[l], conv_w[l], conv_b[l], w_down[l])
        mk_p, mv_p = _mem_kv(mem_prompt, mem_norm[l], w_mem_kv[l])
        pool0 = jnp.zeros((B, POOL_HIST, POOL_WIDTH), x_prompt.dtype)
        conv0 = jnp.zeros((B, CONV_W - 1, 2 * D_FF), x_prompt.dtype)
        hp, k_p, v_p, pool_p, conv_p = _layer(hp, pos_p, None, None, pool0, conv0, mk_p, mv_p,
                                              lam, lam_init, *shared)
        past_k = cache_k[l][page_table].reshape(DB, past, N_HEADS, 2, HEAD_DIM)
        past_v = cache_v[l][page_table].reshape(DB, past, N_HEADS, 2 * HEAD_DIM)
        hs, k_s, v_s, pool_s, conv_s = _layer(hs, pos_s, past_k, past_v, state_pool[l], state_conv[l],
                                              cache_mem_k[l], cache_mem_v[l], lam, lam_init, *shared)
        kp_l.append(k_p); vp_l.append(v_p); ks_l.append(k_s); vs_l.append(v_s)
        plp_l.append(pool_p); pls_l.append(pool_s); cvp_l.append(conv_p); cvs_l.append(conv_s)
        mkp_l.append(mk_p); mvp_l.append(mv_p)
    y_prompt = _rms(hp, final_norm)
    y_sample = _rms(hs, final_norm)
    return (y_prompt, y_sample,
            jnp.stack(kp_l), jnp.stack(vp_l), jnp.stack(ks_l), jnp.stack(vs_l),
            jnp.stack(plp_l), jnp.stack(pls_l), jnp.stack(cvp_l), jnp.stack(cvs_l),
            jnp.stack(mkp_l), jnp.stack(mvp_l))
```

```python
import functools
import math

import jax
import jax.numpy as jnp
from jax import lax
from jax.experimental import pallas as pl
from jax.experimental.pallas import tpu as pltpu

F32 = jnp.float32
BF16 = jnp.bfloat16

D_MODEL = 2048
HEAD_DIM = 128
N_HEADS = 4
DIFF_WIDTH = 1024
POOL_WIDTH = 512
POOL_WINDOWS = (2, 4, 8, 16)
POOL_GROUP_DIM = 128
POOL_HIST = 15
POOL_HIST_PAD = 16
MEM_LEN = 256
MEM_HEADS = 4
MEM_WIDTH = 512
IN_WIDTH = 4096
D_FF = 5632
CONV_W = 3
ROPE_THETA = 10000.0
NORM_EPS = 1e-6
PAGE_SIZE = 128
LAM_INIT = 0.8 - 0.6 * math.exp(-0.3 * 0)
ATTN_SCALE = HEAD_DIM ** -0.5

SUBLANES = 8
VMEM_LIMIT_BYTES = 56 * 1024 * 1024

_NT = (((1,), (1,)), ((), ()))


def _rms(x, g):
    r = lax.rsqrt(jnp.mean(x * x, axis=-1, keepdims=True) + NORM_EPS)
    return x * r * g


def _lam(lq1, lk1, lq2, lk2):
    a = jnp.exp(jnp.sum(lq1[...] * lk1[...], axis=-1, keepdims=True))
    b = jnp.exp(jnp.sum(lq2[...] * lk2[...], axis=-1, keepdims=True))
    return a - b + LAM_INIT


def _params(*sem):
    return pltpu.CompilerParams(dimension_semantics=sem,
                                vmem_limit_bytes=VMEM_LIMIT_BYTES)


def _rms_matmul_kernel(x_ref, g_ref, w_ref, o_ref):
    xn = _rms(x_ref[...], g_ref[...]).astype(BF16)
    o_ref[...] = jnp.dot(xn, w_ref[...], preferred_element_type=F32)


def _rms_matmul(x, g, w_bf, tn):
    r, d = x.shape
    n = w_bf.shape[1]
    return pl.pallas_call(
        _rms_matmul_kernel,
        grid=(n // tn,),
        in_specs=[pl.BlockSpec((r, d), lambda j: (0, 0)),
                  pl.BlockSpec((1, d), lambda j: (0, 0)),
                  pl.BlockSpec((d, tn), lambda j: (0, j))],
        out_specs=pl.BlockSpec((r, tn), lambda j: (0, j)),
        out_shape=jax.ShapeDtypeStruct((r, n), F32),
        compiler_params=_params("arbitrary"),
        name="mem_kv_proj",
    )(x, g.reshape(1, d), w_bf)


def _rope(z, cos, sin):
    parts = []
    for s in range(DIFF_WIDTH // HEAD_DIM):
        zs = z[:, s * HEAD_DIM:(s + 1) * HEAD_DIM]
        parts.append(zs * cos + pltpu.roll(zs, HEAD_DIM // 2, 1) * sin)
    return jnp.concatenate(parts, axis=1)


def _in_proj_kernel(x_ref, g_ref, w_ref, cos_ref, sin_ref,
                    q_ref, kf_ref, kb_ref, vf_ref, vb_ref, p_ref, cq_ref, xn_ref):
    j = pl.program_id(1)

    @pl.when(j == 0)
    def _():
        xn_ref[...] = _rms(x_ref[...], g_ref[...]).astype(BF16)

    z = jnp.dot(xn_ref[...], w_ref[...], preferred_element_type=F32)

    @pl.when(j == 0)
    def _():
        q_ref[...] = _rope(z, cos_ref[...], sin_ref[...]).astype(BF16)

    @pl.when(j == 1)
    def _():
        k = _rope(z, cos_ref[...], sin_ref[...])
        kf_ref[...] = k
        kb_ref[...] = k.astype(BF16)

    @pl.when(j == 2)
    def _():
        vf_ref[...] = z
        vb_ref[...] = z.astype(BF16)

    @pl.when(j == 3)
    def _():
        p_ref[...] = z[:, :POOL_WIDTH]
        cq_ref[...] = z[:, POOL_WIDTH:].astype(BF16)


def _in_proj(x, g, w_bf, cos, sin, tm):
    r = x.shape[0]
    tn = DIFF_WIDTH
    row = lambda i, j: (i, 0)
    wide = lambda dt: jax.ShapeDtypeStruct((r, DIFF_WIDTH), dt)
    half = lambda dt: jax.ShapeDtypeStruct((r, POOL_WIDTH), dt)
    return pl.pallas_call(
        _in_proj_kernel,
        grid=(r // tm, IN_WIDTH // tn),
        in_specs=[pl.BlockSpec((tm, D_MODEL), row),
                  pl.BlockSpec((1, D_MODEL), lambda i, j: (0, 0)),
                  pl.BlockSpec((D_MODEL, tn), lambda i, j: (0, j)),
                  pl.BlockSpec((tm, HEAD_DIM), row),
                  pl.BlockSpec((tm, HEAD_DIM), row)],
        out_specs=[pl.BlockSpec((tm, DIFF_WIDTH), row)] * 5
                  + [pl.BlockSpec((tm, POOL_WIDTH), row)] * 2,
        out_shape=[wide(BF16), wide(F32), wide(BF16), wide(F32), wide(BF16),
                   half(F32), half(BF16)],
        scratch_shapes=[pltpu.VMEM((tm, D_MODEL), BF16)],
        compiler_params=_params("arbitrary", "arbitrary"),
        name="in_proj_rope",
    )(x, g.reshape(1, D_MODEL), w_bf, cos, sin)


def _subln_out(o, g_ref):
    return _rms(o, g_ref[...]) * (1.0 - LAM_INIT)


def _flash_kernel(lq1, lk1, lq2, lk2, q_ref, k_ref, v_ref, g_ref, o_ref,
                  m_ref, l_ref, acc_ref, *, tq):
    i = pl.program_id(1)
    m_ref[...] = jnp.full(m_ref.shape, -jnp.inf, F32)
    l_ref[...] = jnp.zeros(l_ref.shape, F32)
    acc_ref[...] = jnp.zeros(acc_ref.shape, F32)

    def block(j, masked):
        start = pl.multiple_of(j * tq, tq)
        kb = k_ref[pl.ds(start, tq), :]
        vb = v_ref[pl.ds(start, tq), :]
        for mi in range(2):
            cs = slice(mi * HEAD_DIM, (mi + 1) * HEAD_DIM)
            s = lax.dot_general(q_ref[:, cs], kb[:, cs], _NT,
                                preferred_element_type=F32) * ATTN_SCALE
            if masked:
                row = lax.broadcasted_iota(jnp.int32, s.shape, 0)
                col = lax.broadcasted_iota(jnp.int32, s.shape, 1)
                s = jnp.where(col <= row, s, -jnp.inf)
            m_old = m_ref[mi]
            m_new = jnp.maximum(m_old, jnp.max(s, axis=1, keepdims=True))
            alpha = jnp.exp(m_old - m_new)
            p = jnp.exp(s - m_new)
            l_ref[mi] = alpha * l_ref[mi] + jnp.sum(p, axis=1, keepdims=True)
            acc_ref[mi] = alpha * acc_ref[mi] + jnp.dot(
                p.astype(BF16), vb, preferred_element_type=F32)
            m_ref[mi] = m_new

    def body(j, carry):
        block(j, False)
        return carry

    lax.fori_loop(0, i, body, 0)
    block(i, True)

    lam = _lam(lq1, lk1, lq2, lk2)
    o = acc_ref[0] * (1.0 / l_ref[0]) - lam * (acc_ref[1] * (1.0 / l_ref[1]))
    o_ref[...] = _subln_out(o, g_ref).astype(BF16)


def _flash(lams, q, k, v, g, tq):
    s = q.shape[0]
    vw = 2 * HEAD_DIM
    vec = pl.BlockSpec((1, HEAD_DIM), lambda h, i: (0, 0))
    return pl.pallas_call(
        functools.partial(_flash_kernel, tq=tq),
        grid=(N_HEADS, s // tq),
        in_specs=[vec, vec, vec, vec,
                  pl.BlockSpec((tq, vw), lambda h, i: (i, h)),
                  pl.BlockSpec((s, vw), lambda h, i: (0, h)),
                  pl.BlockSpec((s, vw), lambda h, i: (0, h)),
                  pl.BlockSpec((1, vw), lambda h, i: (0, 0))],
        out_specs=pl.BlockSpec((tq, vw), lambda h, i: (i, h)),
        out_shape=jax.ShapeDtypeStruct((s, DIFF_WIDTH), BF16),
        scratch_shapes=[pltpu.VMEM((2, tq, 1), F32),
                        pltpu.VMEM((2, tq, 1), F32),
                        pltpu.VMEM((2, tq, vw), F32)],
        compiler_params=_params("arbitrary", "arbitrary"),
        name="prompt_diff_attn",
    )(*lams, q, k, v, g.reshape(1, vw))


def _decode_kernel(pt_ref, lq1, lk1, lq2, lk2, q_ref, kn_ref, vn_ref, g_ref, *rest,
                   n_pages, n_chunks, n_new):
    del pt_ref
    k_refs = rest[:n_pages]
    v_refs = rest[n_pages:2 * n_pages]
    o_ref = rest[2 * n_pages]
    qbd_ref, kbuf, vbuf, m_ref, l_ref, acc_ref = rest[2 * n_pages + 1:]
    c = pl.program_id(1)
    rows = 2 * n_new * N_HEADS

    @pl.when(c == 0)
    def _():
        r = lax.broadcasted_iota(jnp.int32, (rows, DIFF_WIDTH), 0)
        col = lax.broadcasted_iota(jnp.int32, (rows, DIFF_WIDTH), 1)
        seg = 2 * (r % N_HEADS) + r // (n_new * N_HEADS)
        q = q_ref[0]
        qbd_ref[...] = jnp.where(col // HEAD_DIM == seg, q, jnp.zeros_like(q))
        m_ref[...] = jnp.full(m_ref.shape, -jnp.inf, F32)
        l_ref[...] = jnp.zeros(l_ref.shape, F32)
        acc_ref[...] = jnp.zeros(acc_ref.shape, F32)

    for i in range(n_pages):
        kbuf[i * PAGE_SIZE:(i + 1) * PAGE_SIZE, :] = k_refs[i][0].astype(BF16)
        vbuf[i * PAGE_SIZE:(i + 1) * PAGE_SIZE, :] = v_refs[i][0].astype(BF16)

    s = lax.dot_general(qbd_ref[...], kbuf[...], _NT,
                        preferred_element_type=F32) * ATTN_SCALE
    m_old = m_ref[...]
    m_new = jnp.maximum(m_old, jnp.max(s, axis=1, keepdims=True))
    alpha = jnp.exp(m_old - m_new)
    p = jnp.exp(s - m_new)
    l_ref[...] = alpha * l_ref[...] + jnp.sum(p, axis=1, keepdims=True)
    acc_ref[...] = alpha * acc_ref[...] + jnp.dot(
        p.astype(BF16), vbuf[...], preferred_element_type=F32)
    m_ref[...] = m_new

    @pl.when(c == n_chunks - 1)
    def _():
        qf = qbd_ref[...].astype(F32)
        kn = kn_ref[0].astype(F32)
        vn = vn_ref[0].astype(F32)
        r1 = lax.broadcasted_iota(jnp.int32, (rows, 1), 0)
        tok = (r1 % (n_new * N_HEADS)) // N_HEADS
        s_new = [jnp.sum(qf * kn[t:t + 1, :], axis=1, keepdims=True) * ATTN_SCALE
                 for t in range(n_new)]
        m_old = m_ref[...]
        m_new = m_old
        for t in range(n_new):
            m_new = jnp.where(t <= tok, jnp.maximum(m_new, s_new[t]), m_new)
        alpha = jnp.exp(m_old - m_new)
        l = alpha * l_ref[...]
        acc = alpha * acc_ref[...]
        for t in range(n_new):
            pt = jnp.where(t <= tok, jnp.exp(s_new[t] - m_new), 0.0)
            l = l + pt
            acc = acc + pt.astype(BF16).astype(F32) * vn[t:t + 1, :]
        o = acc * (1.0 / l)
        half = rows // 2
        lam = _lam(lq1, lk1, lq2, lk2)
        o = o[:half] - lam * o[half:]
        head = lax.broadcasted_iota(jnp.int32, (half, 1), 0) % N_HEADS
        vw = 2 * HEAD_DIM
        sel = jnp.zeros((half, vw), F32)
        for h in range(N_HEADS):
            sel = sel + jnp.where(head == h, o[:, h * vw:(h + 1) * vw], 0.0)
        o_ref[0] = _subln_out(sel, g_ref).astype(BF16)


def _decode_attn(page_table, lams, q_rep, k_new, v_new, g, cache_k, cache_v, n_pages):
    nb, rows, _ = q_rep.shape
    n_new = k_new.shape[1]
    n_chunks = page_table.shape[1] // n_pages
    vw = 2 * HEAD_DIM
    vec = pl.BlockSpec((1, HEAD_DIM), lambda b, c, pt: (0, 0))
    per_b = lambda r: pl.BlockSpec((1, r, DIFF_WIDTH), lambda b, c, pt: (b, 0, 0))

    def page_spec(i):
        return pl.BlockSpec((1, PAGE_SIZE, DIFF_WIDTH),
                            lambda b, c, pt: (pt[b, c * n_pages + i], 0, 0))

    pages = [page_spec(i) for i in range(n_pages)]
    grid_spec = pltpu.PrefetchScalarGridSpec(
        num_scalar_prefetch=1,
        grid=(nb, n_chunks),
        in_specs=[vec, vec, vec, vec, per_b(rows), per_b(n_new), per_b(n_new),
                  pl.BlockSpec((1, vw), lambda b, c, pt: (0, 0))] + pages + pages,
        out_specs=pl.BlockSpec((1, rows // 2, vw), lambda b, c, pt: (b, 0, 0)),
        scratch_shapes=[pltpu.VMEM((rows, DIFF_WIDTH), BF16),
                        pltpu.VMEM((n_pages * PAGE_SIZE, DIFF_WIDTH), BF16),
                        pltpu.VMEM((n_pages * PAGE_SIZE, DIFF_WIDTH), BF16),
                        pltpu.VMEM((rows, 1), F32),
                        pltpu.VMEM((rows, 1), F32),
                        pltpu.VMEM((rows, DIFF_WIDTH), F32)])
    return pl.pallas_call(
        functools.partial(_decode_kernel, n_pages=n_pages, n_chunks=n_chunks, n_new=n_new),
        grid_spec=grid_spec,
        out_shape=jax.ShapeDtypeStruct((nb, rows // 2, vw), BF16),
        compiler_params=_params("arbitrary", "arbitrary"),
        name="sample_paged_diff_attn",
    )(page_table, *lams, q_rep, k_new, v_new, g.reshape(1, vw),
      *([cache_k] * n_pages), *([cache_v] * n_pages))


def _mem_attn_kernel(q_ref, mk_ref, mv_ref, o_ref):
    scale = (MEM_WIDTH // MEM_HEADS) ** -0.5
    for h in range(MEM_HEADS):
        cs = slice(h * HEAD_DIM, (h + 1) * HEAD_DIM)
        kh = mk_ref[0, :, cs].astype(BF16)
        vh = mv_ref[0, :, cs].astype(BF16)
        s = lax.dot_general(q_ref[0, :, cs], kh, _NT, preferred_element_type=F32) * scale
        e = jnp.exp(s - jnp.max(s, axis=1, keepdims=True))
        p = e / jnp.sum(e, axis=1, keepdims=True)
        o_ref[0, :, cs] = jnp.dot(p.astype(BF16), vh,
                                  preferred_element_type=F32).astype(BF16)


def _mem_attn(q, mk, mv, tq):
    nb, t, _ = q.shape
    shared = mk.shape[0] == 1
    mem_idx = (lambda b, i: (0, 0, 0)) if shared else (lambda b, i: (b, 0, 0))
    return pl.pallas_call(
        _mem_attn_kernel,
        grid=(nb, t // tq),
        in_specs=[pl.BlockSpec((1, tq, MEM_WIDTH), lambda b, i: (b, i, 0)),
                  pl.BlockSpec((1, MEM_LEN, MEM_WIDTH), mem_idx),
                  pl.BlockSpec((1, MEM_LEN, MEM_WIDTH), mem_idx)],
        out_specs=pl.BlockSpec((1, tq, MEM_WIDTH), lambda b, i: (b, i, 0)),
        out_shape=jax.ShapeDtypeStruct((nb, t, MEM_WIDTH), BF16),
        compiler_params=_params("arbitrary", "arbitrary"),
        name="mem_xattn",
    )(q, mk, mv)


def _mix_kernel(oatt_ref, omem_ref, p_ref, x_ref, hist_ref, wp_ref, ps_ref, wo_ref,
                h_ref, pext_ref, cat_ref, *, tm, bq, pos0, n_i):
    i = pl.program_id(0)
    hr = POOL_HIST_PAD * bq
    if n_i > 1:
        @pl.when(i == 0)
        def _():
            pext_ref[0:hr, :] = hist_ref[...]
    else:
        pext_ref[0:hr, :] = hist_ref[...]
    p = p_ref[...]
    pext_ref[hr:hr + tm, :] = p
    rows = i * tm + lax.broadcasted_iota(jnp.int32, (tm, 1), 0)
    pos = pos0 + rows // bq
    for g, w in enumerate(POOL_WINDOWS):
        cs = slice(g * POOL_GROUP_DIM, (g + 1) * POOL_GROUP_DIM)
        acc = p[:, cs]
        for k in range(1, w):
            acc = acc + pext_ref[hr - k * bq:hr - k * bq + tm, cs]
        cnt = jnp.minimum(pos + 1, w).astype(F32)
        d = acc / cnt - p[:, cs]
        y = jnp.dot(d.astype(BF16), wp_ref[g], preferred_element_type=F32) * ps_ref[:, cs]
        cat_ref[:, DIFF_WIDTH + g * POOL_GROUP_DIM:DIFF_WIDTH + (g + 1) * POOL_GROUP_DIM] = (
            y.astype(BF16))
    cat_ref[:, 0:DIFF_WIDTH] = oatt_ref[...]
    cat_ref[:, DIFF_WIDTH + POOL_WIDTH:] = omem_ref[...]
    h_ref[...] = x_ref[...] + jnp.dot(cat_ref[...], wo_ref[...], preferred_element_type=F32)
    if n_i > 1:
        pext_ref[0:hr, :] = pext_ref[tm:tm + hr, :]


def _mix(oatt, omem, p, x, hist, wp_bf, pool_scale, wo_bf, tm, bq, pos0):
    r = x.shape[0]
    n_i = r // tm
    hr = POOL_HIST_PAD * bq
    row = lambda i: (i, 0)
    fixed = lambda i: (0, 0)
    return pl.pallas_call(
        functools.partial(_mix_kernel, tm=tm, bq=bq, pos0=pos0, n_i=n_i),
        grid=(n_i,),
        in_specs=[pl.BlockSpec((tm, DIFF_WIDTH), row),
                  pl.BlockSpec((tm, MEM_WIDTH), row),
                  pl.BlockSpec((tm, POOL_WIDTH), row),
                  pl.BlockSpec((tm, D_MODEL), row),
                  pl.BlockSpec((hr, POOL_WIDTH), fixed),
                  pl.BlockSpec((len(POOL_WINDOWS), POOL_GROUP_DIM, POOL_GROUP_DIM),
                               lambda i: (0, 0, 0)),
                  pl.BlockSpec((1, POOL_WIDTH), fixed),
                  pl.BlockSpec((D_MODEL, D_MODEL), fixed)],
        out_specs=pl.BlockSpec((tm, D_MODEL), row),
        out_shape=jax.ShapeDtypeStruct((r, D_MODEL), F32),
        scratch_shapes=[pltpu.VMEM((hr + tm, POOL_WIDTH), F32),
                        pltpu.VMEM((tm, D_MODEL), BF16)],
        compiler_params=_params("arbitrary"),
        name="pool_mix_out_proj",
    )(oatt, omem, p, x, hist, wp_bf, pool_scale.reshape(1, POOL_WIDTH), wo_bf)


def _ffn_kernel(h_ref, g_ref, wg_ref, wv_ref, cwg_ref, cwv_ref, cbg_ref, cbv_ref,
                hg_ref, hv_ref, wd_ref, fn_ref,
                y_ref, tg_ref, tv_ref,
                hn_ref, ug_ref, uv_ref, cg_ref, cv_ref, *, tm, bq, head, n_i):
    i = pl.program_id(0)
    j = pl.program_id(1)

    @pl.when(j == 0)
    def _():
        h = h_ref[...]
        hn_ref[...] = _rms(h, g_ref[...]).astype(BF16)
        y_ref[...] = h

    hn = hn_ref[...]

    def conv_half(w_ref, cw_ref, cb_ref, hist_ref, u_ref, carry_ref, tail_ref):
        u = jnp.dot(hn, w_ref[...], preferred_element_type=F32)
        u_ref[head:head + tm, :] = u
        last = u[tm - head:, :]
        if n_i > 1:
            @pl.when(i == 0)
            def _():
                u_ref[0:head, :] = hist_ref[...]

            @pl.when(i > 0)
            def _():
                u_ref[0:head, :] = carry_ref[j]

            carry_ref[j] = last
        else:
            u_ref[0:head, :] = hist_ref[...]
        tail_ref[0] = last
        cw = cw_ref[...]
        c = cb_ref[...] + cw[0:1, :] * u_ref[head - 2 * bq:head - 2 * bq + tm, :]
        c = c + cw[1:2, :] * u_ref[head - bq:head - bq + tm, :]
        return c + cw[2:3, :] * u

    gate = conv_half(wg_ref, cwg_ref, cbg_ref, hg_ref, ug_ref, cg_ref, tg_ref)
    val = conv_half(wv_ref, cwv_ref, cbv_ref, hv_ref, uv_ref, cv_ref, tv_ref)
    act = gate * (1.0 / (1.0 + jnp.exp(-gate))) * val
    y_ref[...] += jnp.dot(act.astype(BF16), wd_ref[...], preferred_element_type=F32)

    @pl.when(j == pl.num_programs(1) - 1)
    def _():
        y_ref[...] = _rms(y_ref[...], fn_ref[...])


def _ffn(h, g, wup_bf, conv_w, conv_b, hist, wd_bf, fn, tm, tf, bq):
    r = h.shape[0]
    n_i = r // tm
    n_j = D_FF // tf
    head = hist.shape[0]
    assert head % SUBLANES == 0 and head >= 2 * bq
    assert n_i == 1 or head == SUBLANES
    row = lambda i, j: (i, 0)
    fixed = lambda i, j: (0, 0)
    gcol = lambda i, j: (0, j)
    vcol = lambda i, j: (0, n_j + j)
    conv_b2 = conv_b.reshape(1, 2 * D_FF)
    y, tg, tv = pl.pallas_call(
        functools.partial(_ffn_kernel, tm=tm, bq=bq, head=head, n_i=n_i),
        grid=(n_i, n_j),
        in_specs=[pl.BlockSpec((tm, D_MODEL), row),
                  pl.BlockSpec((1, D_MODEL), fixed),
                  pl.BlockSpec((D_MODEL, tf), gcol),
                  pl.BlockSpec((D_MODEL, tf), vcol),
                  pl.BlockSpec((CONV_W, tf), gcol),
                  pl.BlockSpec((CONV_W, tf), vcol),
                  pl.BlockSpec((1, tf), gcol),
                  pl.BlockSpec((1, tf), vcol),
                  pl.BlockSpec((head, tf), gcol),
                  pl.BlockSpec((head, tf), vcol),
                  pl.BlockSpec((tf, D_MODEL), lambda i, j: (j, 0)),
                  pl.BlockSpec((1, D_MODEL), fixed)],
        out_specs=[pl.BlockSpec((tm, D_MODEL), row),
                   pl.BlockSpec((1, head, tf), lambda i, j: (i, 0, j)),
                   pl.BlockSpec((1, head, tf), lambda i, j: (i, 0, j))],
        out_shape=[jax.ShapeDtypeStruct((r, D_MODEL), F32),
                   jax.ShapeDtypeStruct((n_i, head, D_FF), F32),
                   jax.ShapeDtypeStruct((n_i, head, D_FF), F32)],
        scratch_shapes=[pltpu.VMEM((tm, D_MODEL), BF16),
                        pltpu.VMEM((head + tm, tf), F32),
                        pltpu.VMEM((head + tm, tf), F32),
                        pltpu.VMEM((n_j, head, tf), F32),
                        pltpu.VMEM((n_j, head, tf), F32)],
        compiler_params=_params("arbitrary", "arbitrary"),
        name="conv_gated_ffn",
    )(h, g.reshape(1, D_MODEL), wup_bf, wup_bf, conv_w, conv_w, conv_b2, conv_b2,
      hist, hist, wd_bf, fn.reshape(1, D_MODEL))
    return y, jnp.concatenate([tg[n_i - 1], tv[n_i - 1]], axis=1)


def _rope_tables(pos):
    half = HEAD_DIM // 2
    inv = ROPE_THETA ** (-jnp.arange(half, dtype=F32) / half)
    ang = pos.astype(F32)[:, None] * inv[None, :]
    cos, sin = jnp.cos(ang), jnp.sin(ang)
    return jnp.concatenate([cos, cos], axis=1), jnp.concatenate([-sin, sin], axis=1)


def kernel(x_prompt, x_sample, cache_k, cache_v, page_table, state_pool, state_conv,
           cache_mem_k, cache_mem_v, mem_prompt, attn_norm, w_in, lam_q1, lam_k1,
           lam_q2, lam_k2, subln_norm, w_pool, pool_scale, mem_norm, w_mem_kv, w_o,
           ffn_norm, w_up, conv_w, conv_b, w_down, final_norm):
    assert w_in.shape[0] == 1 and x_prompt.shape[0] == 1
    _, seq, _ = x_prompt.shape
    nb, n_new, _ = x_sample.shape
    past = page_table.shape[1] * cache_k.shape[2]
    n_phys = cache_k.shape[1]

    w_in_b = w_in[0].astype(BF16)
    w_mem_b = w_mem_kv[0].astype(BF16)
    w_o_b = w_o[0].astype(BF16)
    w_up_b = w_up[0].astype(BF16)
    w_down_b = w_down[0].astype(BF16)
    w_pool_b = w_pool[0].astype(BF16)
    lams = [v[0].reshape(1, HEAD_DIM) for v in (lam_q1, lam_k1, lam_q2, lam_k2)]

    xp = x_prompt.reshape(seq, D_MODEL)
    cos_p, sin_p = _rope_tables(jnp.arange(seq, dtype=jnp.int32))
    q_p, kf_p, kb_p, vf_p, vb_p, p_p, cq_p = _in_proj(
        xp, attn_norm[0], w_in_b, cos_p, sin_p, tm=512)
    zmem = _rms_matmul(mem_prompt[0], mem_norm[0], w_mem_b, tn=MEM_WIDTH)
    mk_p, mv_p = zmem[:, :MEM_WIDTH], zmem[:, MEM_WIDTH:]
    oatt_p = _flash(lams, q_p, kb_p, vb_p, subln_norm[0], tq=512)
    omem_p = _mem_attn(cq_p[None], mk_p[None], mv_p[None], tq=512)[0]
    h_p = _mix(oatt_p, omem_p, p_p, xp, jnp.zeros((POOL_HIST_PAD, POOL_WIDTH), F32),
               w_pool_b, pool_scale[0], w_o_b, tm=512, bq=1, pos0=0)
    y_p, tail_p = _ffn(h_p, ffn_norm[0], w_up_b, conv_w[0], conv_b[0],
                       jnp.zeros((SUBLANES, 2 * D_FF), F32), w_down_b, final_norm,
                       tm=512, tf=512, bq=1)

    rows = nb * n_new
    xs = x_sample.reshape(rows, D_MODEL)
    pos_s = past + jnp.arange(rows, dtype=jnp.int32) % n_new
    cos_s, sin_s = _rope_tables(pos_s)
    q_s, kf_s, kb_s, vf_s, vb_s, p_s, cq_s = _in_proj(
        xs, attn_norm[0], w_in_b, cos_s, sin_s, tm=rows)
    q_rep = jnp.broadcast_to(q_s.reshape(nb, 1, n_new, 1, DIFF_WIDTH),
                             (nb, 2, n_new, N_HEADS, DIFF_WIDTH))
    q_rep = q_rep.reshape(nb, 2 * n_new * N_HEADS, DIFF_WIDTH)
    oatt_s = _decode_attn(
        page_table, lams, q_rep, kb_s.reshape(nb, n_new, DIFF_WIDTH),
        vb_s.reshape(nb, n_new, DIFF_WIDTH), subln_norm[0],
        cache_k[0].reshape(n_phys, PAGE_SIZE, DIFF_WIDTH),
        cache_v[0].reshape(n_phys, PAGE_SIZE, DIFF_WIDTH), n_pages=8)
    oatt_s = oatt_s.reshape(nb, n_new, DIFF_WIDTH)
    cq_pad = jnp.pad(cq_s.reshape(nb, n_new, MEM_WIDTH),
                     ((0, 0), (0, SUBLANES - n_new), (0, 0)))
    omem_s = _mem_attn(cq_pad, cache_mem_k[0].reshape(nb, MEM_LEN, MEM_WIDTH),
                       cache_mem_v[0].reshape(nb, MEM_LEN, MEM_WIDTH),
                       tq=SUBLANES)[:, :n_new]

    def tmaj(a):
        return jnp.swapaxes(a, 0, 1).reshape(a.shape[0] * a.shape[1], a.shape[2])

    def smaj(a, t):
        return jnp.swapaxes(a.reshape(t, nb, a.shape[1]), 0, 1)

    hist_pool = jnp.pad(tmaj(state_pool[0]), ((nb, 0), (0, 0)))
    h_s = _mix(tmaj(oatt_s), tmaj(omem_s), tmaj(p_s.reshape(nb, n_new, POOL_WIDTH)),
               tmaj(x_sample), hist_pool, w_pool_b, pool_scale[0], w_o_b,
               tm=rows, bq=nb, pos0=past)
    y_s, tail_s = _ffn(h_s, ffn_norm[0], w_up_b, conv_w[0], conv_b[0],
                       tmaj(state_conv[0]), w_down_b, final_norm,
                       tm=rows, tf=512, bq=nb)

    kv5 = (N_HEADS, 2, HEAD_DIM)
    pool_s = jnp.concatenate([state_pool[0], p_s.reshape(nb, n_new, POOL_WIDTH)],
                             axis=1)[:, -POOL_HIST:]
    return (
        y_p.reshape(1, seq, D_MODEL),
        smaj(y_s, n_new),
        kf_p.reshape(1, 1, seq, *kv5),
        vf_p.reshape(1, 1, seq, N_HEADS, 2 * HEAD_DIM),
        kf_s.reshape(1, nb, n_new, *kv5),
        vf_s.reshape(1, nb, n_new, N_HEADS, 2 * HEAD_DIM),
        p_p[seq - POOL_HIST:].reshape(1, 1, POOL_HIST, POOL_WIDTH),
        pool_s[None],
        tail_p[SUBLANES - (CONV_W - 1):].reshape(1, 1, CONV_W - 1, 2 * D_FF),
        smaj(tail_s, CONV_W - 1)[None],
        mk_p.reshape(1, 1, MEM_LEN, MEM_HEADS, HEAD_DIM),
        mv_p.reshape(1, 1, MEM_LEN, MEM_HEADS, HEAD_DIM),
    )
```

```python
import functools
import math

import jax
import jax.numpy as jnp
from jax import lax
from jax.experimental import pallas as pl
from jax.experimental.pallas import tpu as pltpu

F32 = jnp.float32
BF16 = jnp.bfloat16

D_MODEL = 2048
HEAD_DIM = 128
N_HEADS = 4
DIFF_WIDTH = 1024
POOL_WIDTH = 512
POOL_WINDOWS = (2, 4, 8, 16)
POOL_GROUP_DIM = 128
POOL_HIST = 15
POOL_HIST_PAD = 16
MEM_LEN = 256
MEM_HEADS = 4
MEM_WIDTH = 512
IN_WIDTH = 4096
D_FF = 5632
CONV_W = 3
ROPE_THETA = 10000.0
NORM_EPS = 1e-6
PAGE_SIZE = 128
LAM_INIT = 0.8 - 0.6 * math.exp(-0.3 * 0)
ATTN_SCALE = HEAD_DIM ** -0.5

EXP2_SCALE = ATTN_SCALE * math.log2(math.e)
KV_GROUP = 4
FFN_SLAB = 256

SUBLANES = 8
LANES = 128
VMEM_LIMIT_BYTES = 56 * 1024 * 1024

_NT = (((1,), (1,)), ((), ()))


def _rms(x, g):
    r = lax.rsqrt(jnp.mean(x * x, axis=-1, keepdims=True) + NORM_EPS)
    return x * r * g


def _lam(lq1, lk1, lq2, lk2):
    a = jnp.exp(jnp.sum(lq1[...] * lk1[...], axis=-1, keepdims=True))
    b = jnp.exp(jnp.sum(lq2[...] * lk2[...], axis=-1, keepdims=True))
    return a - b + LAM_INIT


def _lanes(x, width):
    return jnp.concatenate([x] * (width // LANES), axis=1)


def _params(*sem):
    return pltpu.CompilerParams(dimension_semantics=sem,
                                vmem_limit_bytes=VMEM_LIMIT_BYTES)


def _rms_matmul_kernel(x_ref, g_ref, w_ref, o_ref):
    xn = _rms(x_ref[...], g_ref[...]).astype(BF16)
    o_ref[...] = jnp.dot(xn, w_ref[...], preferred_element_type=F32)


def _rms_matmul(x, g, w_bf, tn):
    r, d = x.shape
    n = w_bf.shape[1]
    return pl.pallas_call(
        _rms_matmul_kernel,
        grid=(n // tn,),
        in_specs=[pl.BlockSpec((r, d), lambda j: (0, 0)),
                  pl.BlockSpec((1, d), lambda j: (0, 0)),
                  pl.BlockSpec((d, tn), lambda j: (0, j))],
        out_specs=pl.BlockSpec((r, tn), lambda j: (0, j)),
        out_shape=jax.ShapeDtypeStruct((r, n), F32),
        compiler_params=_params("arbitrary"),
        name="mem_kv_proj",
    )(x, g.reshape(1, d), w_bf)


def _rope(z, cos, sin):
    parts = []
    for s in range(DIFF_WIDTH // HEAD_DIM):
        zs = z[:, s * HEAD_DIM:(s + 1) * HEAD_DIM]
        parts.append(zs * cos + pltpu.roll(zs, HEAD_DIM // 2, 1) * sin)
    return jnp.concatenate(parts, axis=1)


def _in_proj_kernel(x_ref, g_ref, w_ref, cos_ref, sin_ref,
                    q_ref, kf_ref, kb_ref, vf_ref, vb_ref, p_ref, cq_ref, xn_ref):
    j = pl.program_id(1)

    @pl.when(j == 0)
    def _():
        xn_ref[...] = _rms(x_ref[...], g_ref[...]).astype(BF16)

    z = jnp.dot(xn_ref[...], w_ref[...], preferred_element_type=F32)

    @pl.when(j == 0)
    def _():
        q_ref[...] = _rope(z, cos_ref[...], sin_ref[...]).astype(BF16)

    @pl.when(j == 1)
    def _():
        k = _rope(z, cos_ref[...], sin_ref[...])
        kf_ref[...] = k
        kb_ref[...] = k.astype(BF16)

    @pl.when(j == 2)
    def _():
        vf_ref[...] = z
        vb_ref[...] = z.astype(BF16)

    @pl.when(j == 3)
    def _():
        p_ref[...] = z[:, :POOL_WIDTH]
        cq_ref[...] = z[:, POOL_WIDTH:].astype(BF16)


def _in_proj(x, g, w_bf, cos, sin, tm):
    r = x.shape[0]
    tn = DIFF_WIDTH
    row = lambda i, j: (i, 0)
    wide = lambda dt: jax.ShapeDtypeStruct((r, DIFF_WIDTH), dt)
    half = lambda dt: jax.ShapeDtypeStruct((r, POOL_WIDTH), dt)
    return pl.pallas_call(
        _in_proj_kernel,
        grid=(r // tm, IN_WIDTH // tn),
        in_specs=[pl.BlockSpec((tm, D_MODEL), row),
                  pl.BlockSpec((1, D_MODEL), lambda i, j: (0, 0)),
                  pl.BlockSpec((D_MODEL, tn), lambda i, j: (0, j)),
                  pl.BlockSpec((tm, HEAD_DIM), row),
                  pl.BlockSpec((tm, HEAD_DIM), row)],
        out_specs=[pl.BlockSpec((tm, DIFF_WIDTH), row)] * 5
                  + [pl.BlockSpec((tm, POOL_WIDTH), row)] * 2,
        out_shape=[wide(BF16), wide(F32), wide(BF16), wide(F32), wide(BF16),
                   half(F32), half(BF16)],
        scratch_shapes=[pltpu.VMEM((tm, D_MODEL), BF16)],
        compiler_params=_params("arbitrary", "arbitrary"),
        name="in_proj_rope",
    )(x, g.reshape(1, D_MODEL), w_bf, cos, sin)


def _subln_out(o, g_ref):
    return _rms(o, g_ref[...]) * (1.0 - LAM_INIT)


def _flash_kernel(lq1, lk1, lq2, lk2, q_ref, k_ref, v_ref, g_ref, o_ref,
                  m_ref, l_ref, acc_ref, *, tq):
    i = pl.program_id(1)
    m_ref[...] = jnp.full(m_ref.shape, -jnp.inf, F32)
    l_ref[...] = jnp.zeros(l_ref.shape, F32)
    acc_ref[...] = jnp.zeros(acc_ref.shape, F32)

    def block(j, masked):
        start = pl.multiple_of(j * tq, tq)
        kb = k_ref[pl.ds(start, tq), :]
        vb = v_ref[pl.ds(start, tq), :]
        for mi in range(2):
            cs = slice(mi * HEAD_DIM, (mi + 1) * HEAD_DIM)
            s = lax.dot_general(q_ref[:, cs], kb[:, cs], _NT,
                                preferred_element_type=F32)
            if masked:
                row = lax.broadcasted_iota(jnp.int32, s.shape, 0)
                col = lax.broadcasted_iota(jnp.int32, s.shape, 1)
                s = jnp.where(col - row <= (i - j) * tq, s, -jnp.inf)
            m_old = m_ref[mi]
            m_new = jnp.maximum(m_old, jnp.max(s, axis=1, keepdims=True))
            alpha = jnp.exp2((m_old - m_new) * EXP2_SCALE)
            p = jnp.exp2((s - _lanes(m_new, tq)) * EXP2_SCALE)
            l_ref[mi] = alpha * l_ref[mi] + jnp.sum(p, axis=1, keepdims=True)
            acc_ref[mi] = _lanes(alpha, 2 * HEAD_DIM) * acc_ref[mi] + jnp.dot(
                p.astype(BF16), vb, preferred_element_type=F32)
            m_ref[mi] = m_new

    def group(g, carry):
        for u in range(KV_GROUP):
            block(g * KV_GROUP + u, False)
        return carry

    def single(j, carry):
        block(j, True)
        return carry

    n_full = i // KV_GROUP
    lax.fori_loop(0, n_full, group, 0)
    lax.fori_loop(n_full * KV_GROUP, i + 1, single, 0)

    lam = _lam(lq1, lk1, lq2, lk2)
    inv1 = _lanes(1.0 / l_ref[0], 2 * HEAD_DIM)
    inv2 = _lanes(1.0 / l_ref[1], 2 * HEAD_DIM)
    o = acc_ref[0] * inv1 - lam * (acc_ref[1] * inv2)
    o_ref[...] = _subln_out(o, g_ref).astype(BF16)


def _flash(lams, q, k, v, g, tq):
    s = q.shape[0]
    vw = 2 * HEAD_DIM
    vec = pl.BlockSpec((1, HEAD_DIM), lambda h, i: (0, 0))
    return pl.pallas_call(
        functools.partial(_flash_kernel, tq=tq),
        grid=(N_HEADS, s // tq),
        in_specs=[vec, vec, vec, vec,
                  pl.BlockSpec((tq, vw), lambda h, i: (i, h)),
                  pl.BlockSpec((s, vw), lambda h, i: (0, h)),
                  pl.BlockSpec((s, vw), lambda h, i: (0, h)),
                  pl.BlockSpec((1, vw), lambda h, i: (0, 0))],
        out_specs=pl.BlockSpec((tq, vw), lambda h, i: (i, h)),
        out_shape=jax.ShapeDtypeStruct((s, DIFF_WIDTH), BF16),
        scratch_shapes=[pltpu.VMEM((2, tq, LANES), F32),
                        pltpu.VMEM((2, tq, LANES), F32),
                        pltpu.VMEM((2, tq, vw), F32)],
        compiler_params=_params("arbitrary", "arbitrary"),
        name="prompt_diff_attn",
    )(*lams, q, k, v, g.reshape(1, vw))


def _decode_kernel(pt_ref, lq1, lk1, lq2, lk2, q_ref, kn_ref, vn_ref, g_ref, *rest,
                   n_pages, n_chunks, n_new):
    del pt_ref
    k_refs = rest[:n_pages]
    v_refs = rest[n_pages:2 * n_pages]
    o_ref = rest[2 * n_pages]
    kbuf, vbuf, m_ref, l_ref, acc_ref = rest[2 * n_pages + 1:]
    c = pl.program_id(1)
    half = n_new * N_HEADS
    slots = PAGE_SIZE * N_HEADS

    @pl.when(c == 0)
    def _():
        m_ref[...] = jnp.full(m_ref.shape, -jnp.inf, F32)
        l_ref[...] = jnp.zeros(l_ref.shape, F32)
        acc_ref[...] = jnp.zeros(acc_ref.shape, F32)

    def attend(k_of_map, v, causal):
        keys = v.shape[0]
        row = lax.broadcasted_iota(jnp.int32, (half, keys), 0)
        col = lax.broadcasted_iota(jnp.int32, (half, keys), 1)
        ok = row % N_HEADS == col % N_HEADS
        if causal:
            ok = jnp.logical_and(ok, col // N_HEADS <= row // N_HEADS)
        s = jnp.concatenate(
            [jnp.where(ok, lax.dot_general(q_ref[0, m], k_of_map(m), _NT,
                                           preferred_element_type=F32) * ATTN_SCALE,
                       -jnp.inf) for m in range(2)], axis=0)
        m_old = m_ref[...]
        m_new = jnp.maximum(m_old, jnp.max(s, axis=1, keepdims=True))
        alpha = jnp.exp(m_old - m_new)
        p = jnp.exp(s - m_new)
        l_ref[...] = alpha * l_ref[...] + jnp.sum(p, axis=1, keepdims=True)
        acc_ref[...] = alpha * acc_ref[...] + jnp.dot(
            p.astype(BF16), v, preferred_element_type=F32)
        m_ref[...] = m_new

    for i in range(n_pages):
        dst = slice(i * slots, (i + 1) * slots)
        for m in range(2):
            kbuf[m, dst, :] = k_refs[i][0, pl.ds(m, slots, stride=2), :].astype(BF16)
        vbuf[dst, :] = v_refs[i][0, 0].reshape(slots, 2 * HEAD_DIM).astype(BF16)

    attend(lambda m: kbuf[m], vbuf[...], causal=False)

    @pl.when(c == n_chunks - 1)
    def _():
        attend(lambda m: kn_ref[0, m], vn_ref[0], causal=True)
        o = acc_ref[...] * (1.0 / l_ref[...])
        lam = _lam(lq1, lk1, lq2, lk2)
        o_ref[0] = _subln_out(o[:half] - lam * o[half:], g_ref).astype(BF16)


def _decode_attn(page_table, lams, q, k_new, v_new, g, cache_k, cache_v, n_pages):
    nb, _, half, _ = q.shape
    n_chunks = page_table.shape[1] // n_pages
    vw = 2 * HEAD_DIM
    slots = PAGE_SIZE * N_HEADS
    vec = pl.BlockSpec((1, HEAD_DIM), lambda b, c, pt: (0, 0))
    qk = pl.BlockSpec((1, 2, half, HEAD_DIM), lambda b, c, pt: (b, 0, 0, 0))

    def k_page(i):
        return pl.BlockSpec((1, 2 * slots, HEAD_DIM),
                            lambda b, c, pt: (pt[b, c * n_pages + i], 0, 0))

    def v_page(i):
        return pl.BlockSpec((1, 1, PAGE_SIZE, N_HEADS, vw),
                            lambda b, c, pt: (0, pt[b, c * n_pages + i], 0, 0, 0))

    grid_spec = pltpu.PrefetchScalarGridSpec(
        num_scalar_prefetch=1,
        grid=(nb, n_chunks),
        in_specs=[vec, vec, vec, vec, qk, qk,
                  pl.BlockSpec((1, half, vw), lambda b, c, pt: (b, 0, 0)),
                  pl.BlockSpec((1, vw), lambda b, c, pt: (0, 0))]
                 + [k_page(i) for i in range(n_pages)]
                 + [v_page(i) for i in range(n_pages)],
        out_specs=pl.BlockSpec((1, half, vw), lambda b, c, pt: (b, 0, 0)),
        scratch_shapes=[pltpu.VMEM((2, n_pages * slots, HEAD_DIM), BF16),
                        pltpu.VMEM((n_pages * slots, vw), BF16),
                        pltpu.VMEM((2 * half, 1), F32),
                        pltpu.VMEM((2 * half, 1), F32),
                        pltpu.VMEM((2 * half, vw), F32)])
    return pl.pallas_call(
        functools.partial(_decode_kernel, n_pages=n_pages, n_chunks=n_chunks,
                          n_new=half // N_HEADS),
        grid_spec=grid_spec,
        out_shape=jax.ShapeDtypeStruct((nb, half, vw), BF16),
        compiler_params=_params("arbitrary", "arbitrary"),
        name="sample_paged_diff_attn",
    )(page_table, *lams, q, k_new, v_new, g.reshape(1, vw),
      *([cache_k] * n_pages), *([cache_v] * n_pages))


def _mem_attn_kernel(q_ref, mk_ref, mv_ref, o_ref):
    scale = (MEM_WIDTH // MEM_HEADS) ** -0.5
    for h in range(MEM_HEADS):
        cs = slice(h * HEAD_DIM, (h + 1) * HEAD_DIM)
        kh = mk_ref[0, :, cs].astype(BF16)
        vh = mv_ref[0, :, cs].astype(BF16)
        s = lax.dot_general(q_ref[0, :, cs], kh, _NT, preferred_element_type=F32) * scale
        e = jnp.exp(s - jnp.max(s, axis=1, keepdims=True))
        p = e / jnp.sum(e, axis=1, keepdims=True)
        o_ref[0, :, cs] = jnp.dot(p.astype(BF16), vh,
                                  preferred_element_type=F32).astype(BF16)


def _mem_attn(q, mk, mv, tq):
    nb, t, _ = q.shape
    shared = mk.shape[0] == 1
    mem_idx = (lambda b, i: (0, 0, 0)) if shared else (lambda b, i: (b, 0, 0))
    return pl.pallas_call(
        _mem_attn_kernel,
        grid=(nb, t // tq),
        in_specs=[pl.BlockSpec((1, tq, MEM_WIDTH), lambda b, i: (b, i, 0)),
                  pl.BlockSpec((1, MEM_LEN, MEM_WIDTH), mem_idx),
                  pl.BlockSpec((1, MEM_LEN, MEM_WIDTH), mem_idx)],
        out_specs=pl.BlockSpec((1, tq, MEM_WIDTH), lambda b, i: (b, i, 0)),
        out_shape=jax.ShapeDtypeStruct((nb, t, MEM_WIDTH), BF16),
        compiler_params=_params("arbitrary", "arbitrary"),
        name="mem_xattn",
    )(q, mk, mv)


def _mix_kernel(oatt_ref, omem_ref, p_ref, x_ref, hist_ref, wp_ref, ps_ref, wo_ref,
                h_ref, pext_ref, cat_ref, *, tm, bq, pos0, n_i):
    i = pl.program_id(0)
    hr = POOL_HIST_PAD * bq
    if n_i > 1:
        @pl.when(i == 0)
        def _():
            pext_ref[0:hr, :] = hist_ref[...]
    else:
        pext_ref[0:hr, :] = hist_ref[...]
    p = p_ref[...]
    pext_ref[hr:hr + tm, :] = p
    rows = i * tm + lax.broadcasted_iota(jnp.int32, (tm, 1), 0)
    pos = pos0 + rows // bq
    for g, w in enumerate(POOL_WINDOWS):
        cs = slice(g * POOL_GROUP_DIM, (g + 1) * POOL_GROUP_DIM)
        acc = p[:, cs]
        for k in range(1, w):
            acc = acc + pext_ref[hr - k * bq:hr - k * bq + tm, cs]
        cnt = jnp.minimum(pos + 1, w).astype(F32)
        d = acc / cnt - p[:, cs]
        y = jnp.dot(d.astype(BF16), wp_ref[g], preferred_element_type=F32) * ps_ref[:, cs]
        cat_ref[:, DIFF_WIDTH + g * POOL_GROUP_DIM:DIFF_WIDTH + (g + 1) * POOL_GROUP_DIM] = (
            y.astype(BF16))
    cat_ref[:, 0:DIFF_WIDTH] = oatt_ref[...]
    cat_ref[:, DIFF_WIDTH + POOL_WIDTH:] = omem_ref[...]
    h_ref[...] = x_ref[...] + jnp.dot(cat_ref[...], wo_ref[...], preferred_element_type=F32)
    if n_i > 1:
        pext_ref[0:hr, :] = pext_ref[tm:tm + hr, :]


def _mix(oatt, omem, p, x, hist, wp_bf, pool_scale, wo_bf, tm, bq, pos0):
    r = x.shape[0]
    n_i = r // tm
    hr = POOL_HIST_PAD * bq
    row = lambda i: (i, 0)
    fixed = lambda i: (0, 0)
    return pl.pallas_call(
        functools.partial(_mix_kernel, tm=tm, bq=bq, pos0=pos0, n_i=n_i),
        grid=(n_i,),
        in_specs=[pl.BlockSpec((tm, DIFF_WIDTH), row),
                  pl.BlockSpec((tm, MEM_WIDTH), row),
                  pl.BlockSpec((tm, POOL_WIDTH), row),
                  pl.BlockSpec((tm, D_MODEL), row),
                  pl.BlockSpec((hr, POOL_WIDTH), fixed),
                  pl.BlockSpec((len(POOL_WINDOWS), POOL_GROUP_DIM, POOL_GROUP_DIM),
                               lambda i: (0, 0, 0)),
                  pl.BlockSpec((1, POOL_WIDTH), fixed),
                  pl.BlockSpec((D_MODEL, D_MODEL), fixed)],
        out_specs=pl.BlockSpec((tm, D_MODEL), row),
        out_shape=jax.ShapeDtypeStruct((r, D_MODEL), F32),
        scratch_shapes=[pltpu.VMEM((hr + tm, POOL_WIDTH), F32),
                        pltpu.VMEM((tm, D_MODEL), BF16)],
        compiler_params=_params("arbitrary"),
        name="pool_mix_out_proj",
    )(oatt, omem, p, x, hist, wp_bf, pool_scale.reshape(1, POOL_WIDTH), wo_bf)


def _ffn_kernel(h_ref, g_ref, wg_ref, wv_ref, cwg_ref, cwv_ref, cbg_ref, cbv_ref,
                hg_ref, hv_ref, wd_ref, fn_ref,
                y_ref, tg_ref, tv_ref,
                hn_ref, ug_ref, uv_ref, cg_ref, cv_ref, *, tm, bq, head, n_i):
    i = pl.program_id(0)
    j = pl.program_id(1)

    @pl.when(j == 0)
    def _():
        h = h_ref[...]
        hn_ref[...] = _rms(h, g_ref[...]).astype(BF16)
        y_ref[...] = h

    if n_i > 1:
        @pl.when(i == 0)
        def _():
            cg_ref[j] = hg_ref[...]
            cv_ref[j] = hv_ref[...]

    hn = hn_ref[...]

    def up_proj(w_ref, hist_ref, u_ref, carry_ref, tail_ref):
        u = jnp.dot(hn, w_ref[...], preferred_element_type=F32)
        u_ref[head:head + tm, :] = u
        last = u[tm - head:, :]
        if n_i > 1:
            u_ref[0:head, :] = carry_ref[j]
            carry_ref[j] = last
        else:
            u_ref[0:head, :] = hist_ref[...]
        tail_ref[0] = last

    def conv(cw_ref, cb_ref, u_ref, cs):
        c = cb_ref[:, cs] + cw_ref[0:1, cs] * u_ref[head - 2 * bq:head - 2 * bq + tm, cs]
        c = c + cw_ref[1:2, cs] * u_ref[head - bq:head - bq + tm, cs]
        return c + cw_ref[2:3, cs] * u_ref[head:head + tm, cs]

    up_proj(wg_ref, hg_ref, ug_ref, cg_ref, tg_ref)
    up_proj(wv_ref, hv_ref, uv_ref, cv_ref, tv_ref)
    tf = ug_ref.shape[1]
    down = None
    for s in range(tf // FFN_SLAB):
        cs = slice(s * FFN_SLAB, (s + 1) * FFN_SLAB)
        gate = conv(cwg_ref, cbg_ref, ug_ref, cs)
        val = conv(cwv_ref, cbv_ref, uv_ref, cs)
        act = gate * (1.0 / (1.0 + jnp.exp(-gate))) * val
        part = jnp.dot(act.astype(BF16), wd_ref[cs, :], preferred_element_type=F32)
        down = part if down is None else down + part
    y_ref[...] += down

    @pl.when(j == pl.num_programs(1) - 1)
    def _():
        y_ref[...] = _rms(y_ref[...], fn_ref[...])


def _ffn(h, g, wup_bf, conv_w, conv_b, hist, wd_bf, fn, tm, tf, bq):
    r = h.shape[0]
    n_i = r // tm
    n_j = D_FF // tf
    head = hist.shape[0]
    assert head % SUBLANES == 0 and head >= 2 * bq
    assert n_i == 1 or head == SUBLANES
    row = lambda i, j: (i, 0)
    fixed = lambda i, j: (0, 0)
    gcol = lambda i, j: (0, j)
    vcol = lambda i, j: (0, n_j + j)
    conv_b2 = conv_b.reshape(1, 2 * D_FF)
    y, tg, tv = pl.pallas_call(
        functools.partial(_ffn_kernel, tm=tm, bq=bq, head=head, n_i=n_i),
        grid=(n_i, n_j),
        in_specs=[pl.BlockSpec((tm, D_MODEL), row),
                  pl.BlockSpec((1, D_MODEL), fixed),
                  pl.BlockSpec((D_MODEL, tf), gcol),
                  pl.BlockSpec((D_MODEL, tf), vcol),
                  pl.BlockSpec((CONV_W, tf), gcol),
                  pl.BlockSpec((CONV_W, tf), vcol),
                  pl.BlockSpec((1, tf), gcol),
                  pl.BlockSpec((1, tf), vcol),
                  pl.BlockSpec((head, tf), gcol),
                  pl.BlockSpec((head, tf), vcol),
                  pl.BlockSpec((tf, D_MODEL), lambda i, j: (j, 0)),
                  pl.BlockSpec((1, D_MODEL), fixed)],
        out_specs=[pl.BlockSpec((tm, D_MODEL), row),
                   pl.BlockSpec((1, head, tf), lambda i, j: (i, 0, j)),
                   pl.BlockSpec((1, head, tf), lambda i, j: (i, 0, j))],
        out_shape=[jax.ShapeDtypeStruct((r, D_MODEL), F32),
                   jax.ShapeDtypeStruct((n_i, head, D_FF), F32),
                   jax.ShapeDtypeStruct((n_i, head, D_FF), F32)],
        scratch_shapes=[pltpu.VMEM((tm, D_MODEL), BF16),
                        pltpu.VMEM((head + tm, tf), F32),
                        pltpu.VMEM((head + tm, tf), F32),
                        pltpu.VMEM((n_j, head, tf), F32),
                        pltpu.VMEM((n_j, head, tf), F32)],
        compiler_params=_params("arbitrary", "arbitrary"),
        name="conv_gated_ffn",
    )(h, g.reshape(1, D_MODEL), wup_bf, wup_bf, conv_w, conv_w, conv_b2, conv_b2,
      hist, hist, wd_bf, fn.reshape(1, D_MODEL))
    return y, jnp.concatenate([tg[n_i - 1], tv[n_i - 1]], axis=1)


def _rope_tables(pos):
    half = HEAD_DIM // 2
    inv = ROPE_THETA ** (-jnp.arange(half, dtype=F32) / half)
    ang = pos.astype(F32)[:, None] * inv[None, :]
    cos, sin = jnp.cos(ang), jnp.sin(ang)
    return jnp.concatenate([cos, cos], axis=1), jnp.concatenate([-sin, sin], axis=1)


def kernel(x_prompt, x_sample, cache_k, cache_v, page_table, state_pool, state_conv,
           cache_mem_k, cache_mem_v, mem_prompt, attn_norm, w_in, lam_q1, lam_k1,
           lam_q2, lam_k2, subln_norm, w_pool, pool_scale, mem_norm, w_mem_kv, w_o,
           ffn_norm, w_up, conv_w, conv_b, w_down, final_norm):
    assert w_in.shape[0] == 1 and x_prompt.shape[0] == 1
    _, seq, _ = x_prompt.shape
    nb, n_new, _ = x_sample.shape
    past = page_table.shape[1] * cache_k.shape[2]
    n_phys = cache_k.shape[1]

    w_in_b = w_in[0].astype(BF16)
    w_mem_b = w_mem_kv[0].astype(BF16)
    w_o_b = w_o[0].astype(BF16)
    w_up_b = w_up[0].astype(BF16)
    w_down_b = w_down[0].astype(BF16)
    w_pool_b = w_pool[0].astype(BF16)
    lams = [v[0].reshape(1, HEAD_DIM) for v in (lam_q1, lam_k1, lam_q2, lam_k2)]

    xp = x_prompt.reshape(seq, D_MODEL)
    cos_p, sin_p = _rope_tables(jnp.arange(seq, dtype=jnp.int32))
    q_p, kf_p, kb_p, vf_p, vb_p, p_p, cq_p = _in_proj(
        xp, attn_norm[0], w_in_b, cos_p, sin_p, tm=512)
    zmem = _rms_matmul(mem_prompt[0], mem_norm[0], w_mem_b, tn=MEM_WIDTH)
    mk_p, mv_p = zmem[:, :MEM_WIDTH], zmem[:, MEM_WIDTH:]
    oatt_p = _flash(lams, q_p, kb_p, vb_p, subln_norm[0], tq=512)
    omem_p = _mem_attn(cq_p[None], mk_p[None], mv_p[None], tq=512)[0]
    h_p = _mix(oatt_p, omem_p, p_p, xp, jnp.zeros((POOL_HIST_PAD, POOL_WIDTH), F32),
               w_pool_b, pool_scale[0], w_o_b, tm=512, bq=1, pos0=0)
    y_p, tail_p = _ffn(h_p, ffn_norm[0], w_up_b, conv_w[0], conv_b[0],
                       jnp.zeros((SUBLANES, 2 * D_FF), F32), w_down_b, final_norm,
                       tm=512, tf=512, bq=1)

    rows = nb * n_new
    xs = x_sample.reshape(rows, D_MODEL)
    pos_s = past + jnp.arange(rows, dtype=jnp.int32) % n_new
    cos_s, sin_s = _rope_tables(pos_s)
    q_s, kf_s, kb_s, vf_s, vb_s, p_s, cq_s = _in_proj(
        xs, attn_norm[0], w_in_b, cos_s, sin_s, tm=rows)
    def by_map(a):
        a = a.reshape(nb, n_new, N_HEADS, 2, HEAD_DIM)
        return jnp.transpose(a, (0, 3, 1, 2, 4)).reshape(nb, 2, n_new * N_HEADS, HEAD_DIM)

    oatt_s = _decode_attn(
        page_table, lams, by_map(q_s), by_map(kb_s),
        vb_s.reshape(nb, n_new * N_HEADS, 2 * HEAD_DIM), subln_norm[0],
        cache_k.reshape(n_phys, PAGE_SIZE * N_HEADS * 2, HEAD_DIM), cache_v, n_pages=8)
    oatt_s = oatt_s.reshape(nb, n_new, DIFF_WIDTH)
    cq_pad = jnp.pad(cq_s.reshape(nb, n_new, MEM_WIDTH),
                     ((0, 0), (0, SUBLANES - n_new), (0, 0)))
    omem_s = _mem_attn(cq_pad, cache_mem_k[0].reshape(nb, MEM_LEN, MEM_WIDTH),
                       cache_mem_v[0].reshape(nb, MEM_LEN, MEM_WIDTH),
                       tq=SUBLANES)[:, :n_new]

    def tmaj(a):
        return jnp.swapaxes(a, 0, 1).reshape(a.shape[0] * a.shape[1], a.shape[2])

    def smaj(a, t):
        return jnp.swapaxes(a.reshape(t, nb, a.shape[1]), 0, 1)

    hist_pool = jnp.pad(tmaj(state_pool[0]), ((nb, 0), (0, 0)))
    h_s = _mix(tmaj(oatt_s), tmaj(omem_s), tmaj(p_s.reshape(nb, n_new, POOL_WIDTH)),
               tmaj(x_sample), hist_pool, w_pool_b, pool_scale[0], w_o_b,
               tm=rows, bq=nb, pos0=past)
    y_s, tail_s = _ffn(h_s, ffn_norm[0], w_up_b, conv_w[0], conv_b[0],
                       tmaj(state_conv[0]), w_down_b, final_norm,
                       tm=rows, tf=512, bq=nb)

    kv5 = (N_HEADS, 2, HEAD_DIM)
    pool_s = jnp.concatenate([state_pool[0], p_s.reshape(nb, n_new, POOL_WIDTH)],
                             axis=1)[:, -POOL_HIST:]
    return (
        y_p.reshape(1, seq, D_MODEL),
        smaj(y_s, n_new),
        kf_p.reshape(1, 1, seq, *kv5),
        vf_p.reshape(1, 1, seq, N_HEADS, 2 * HEAD_DIM),
        kf_s.reshape(1, nb, n_new, *kv5),
        vf_s.reshape(1, nb, n_new, N_HEADS, 2 * HEAD_DIM),
        p_p[seq - POOL_HIST:].reshape(1, 1, POOL_HIST, POOL_WIDTH),
        pool_s[None],
        tail_p[SUBLANES - (CONV_W - 1):].reshape(1, 1, CONV_W - 1, 2 * D_FF),
        smaj(tail_s, CONV_W - 1)[None],
        mk_p.reshape(1, 1, MEM_LEN, MEM_HEADS, HEAD_DIM),
        mv_p.reshape(1, 1, MEM_LEN, MEM_HEADS, HEAD_DIM),
    )
```

```python
import functools
import math

import jax
import jax.numpy as jnp
from jax import lax
from jax.experimental import pallas as pl
from jax.experimental.pallas import tpu as pltpu

F32 = jnp.float32
BF16 = jnp.bfloat16

D_MODEL = 2048
HEAD_DIM = 128
N_HEADS = 4
DIFF_WIDTH = 1024
POOL_WIDTH = 512
POOL_WINDOWS = (2, 4, 8, 16)
POOL_GROUP_DIM = 128
POOL_HIST = 15
POOL_HIST_PAD = 16
MEM_LEN = 256
MEM_HEADS = 4
MEM_WIDTH = 512
IN_WIDTH = 4096
D_FF = 5632
CONV_W = 3
ROPE_THETA = 10000.0
NORM_EPS = 1e-6
PAGE_SIZE = 128
LAM_INIT = 0.8 - 0.6 * math.exp(-0.3 * 0)
ATTN_SCALE = HEAD_DIM ** -0.5

EXP2_SCALE = ATTN_SCALE * math.log2(math.e)
KV_GROUP = 4
FFN_SLAB = 256
DEC_PAGES = 8

SUBLANES = 8
LANES = 128
VMEM_LIMIT_BYTES = 56 * 1024 * 1024

_NT = (((1,), (1,)), ((), ()))


def _rms(x, g):
    r = lax.rsqrt(jnp.mean(x * x, axis=-1, keepdims=True) + NORM_EPS)
    return x * r * g


def _lam(lq1, lk1, lq2, lk2):
    a = jnp.exp(jnp.sum(lq1[...] * lk1[...], axis=-1, keepdims=True))
    b = jnp.exp(jnp.sum(lq2[...] * lk2[...], axis=-1, keepdims=True))
    return a - b + LAM_INIT


def _lanes(x, width):
    return jnp.concatenate([x] * (width // LANES), axis=1)


def _params(*sem):
    return pltpu.CompilerParams(dimension_semantics=sem,
                                vmem_limit_bytes=VMEM_LIMIT_BYTES)


def _rms_matmul_kernel(x_ref, g_ref, w_ref, o_ref):
    xn = _rms(x_ref[...], g_ref[...]).astype(BF16)
    o_ref[...] = jnp.dot(xn, w_ref[...], preferred_element_type=F32)


def _rms_matmul(x, g, w_bf, tn):
    r, d = x.shape
    n = w_bf.shape[1]
    return pl.pallas_call(
        _rms_matmul_kernel,
        grid=(n // tn,),
        in_specs=[pl.BlockSpec((r, d), lambda j: (0, 0)),
                  pl.BlockSpec((1, d), lambda j: (0, 0)),
                  pl.BlockSpec((d, tn), lambda j: (0, j))],
        out_specs=pl.BlockSpec((r, tn), lambda j: (0, j)),
        out_shape=jax.ShapeDtypeStruct((r, n), F32),
        compiler_params=_params("arbitrary"),
        name="mem_kv_proj",
    )(x, g.reshape(1, d), w_bf)


def _rope(z, cos, sin):
    parts = []
    for s in range(DIFF_WIDTH // HEAD_DIM):
        zs = z[:, s * HEAD_DIM:(s + 1) * HEAD_DIM]
        parts.append(zs * cos + pltpu.roll(zs, HEAD_DIM // 2, 1) * sin)
    return jnp.concatenate(parts, axis=1)


def _in_proj_kernel(x_ref, g_ref, w_ref, cos_ref, sin_ref,
                    q_ref, kf_ref, kb_ref, vf_ref, vb_ref, p_ref, cq_ref, xn_ref):
    j = pl.program_id(1)

    @pl.when(j == 0)
    def _():
        xn_ref[...] = _rms(x_ref[...], g_ref[...]).astype(BF16)

    z = jnp.dot(xn_ref[...], w_ref[...], preferred_element_type=F32)

    @pl.when(j == 0)
    def _():
        q_ref[...] = _rope(z, cos_ref[...], sin_ref[...]).astype(BF16)

    @pl.when(j == 1)
    def _():
        k = _rope(z, cos_ref[...], sin_ref[...])
        kf_ref[0, 0] = k.reshape(kf_ref.shape[2:])
        kb_ref[...] = k.astype(BF16)

    @pl.when(j == 2)
    def _():
        vf_ref[0, 0] = z.reshape(vf_ref.shape[2:])
        vb_ref[...] = z.astype(BF16)

    @pl.when(j == 3)
    def _():
        p_ref[...] = z[:, :POOL_WIDTH]
        cq_ref[...] = z[:, POOL_WIDTH:].astype(BF16)


def _in_proj(x, g, w_bf, cos, sin, tm):
    r = x.shape[0]
    tn = DIFF_WIDTH
    row = lambda i, j: (i, 0)
    wide = lambda dt: jax.ShapeDtypeStruct((r, DIFF_WIDTH), dt)
    half = lambda dt: jax.ShapeDtypeStruct((r, POOL_WIDTH), dt)
    k_dims = (N_HEADS, 2, HEAD_DIM)
    v_dims = (N_HEADS, 2 * HEAD_DIM)
    wide_spec = pl.BlockSpec((tm, DIFF_WIDTH), row)
    half_spec = pl.BlockSpec((tm, POOL_WIDTH), row)
    return pl.pallas_call(
        _in_proj_kernel,
        grid=(r // tm, IN_WIDTH // tn),
        in_specs=[pl.BlockSpec((tm, D_MODEL), row),
                  pl.BlockSpec((1, D_MODEL), lambda i, j: (0, 0)),
                  pl.BlockSpec((D_MODEL, tn), lambda i, j: (0, j)),
                  pl.BlockSpec((tm, HEAD_DIM), row),
                  pl.BlockSpec((tm, HEAD_DIM), row)],
        out_specs=[wide_spec,
                   pl.BlockSpec((1, 1, tm) + k_dims, lambda i, j: (0, 0, i, 0, 0, 0)),
                   wide_spec,
                   pl.BlockSpec((1, 1, tm) + v_dims, lambda i, j: (0, 0, i, 0, 0)),
                   wide_spec, half_spec, half_spec],
        out_shape=[wide(BF16), jax.ShapeDtypeStruct((1, 1, r) + k_dims, F32), wide(BF16),
                   jax.ShapeDtypeStruct((1, 1, r) + v_dims, F32), wide(BF16),
                   half(F32), half(BF16)],
        scratch_shapes=[pltpu.VMEM((tm, D_MODEL), BF16)],
        compiler_params=_params("arbitrary", "arbitrary"),
        name="in_proj_rope",
    )(x, g.reshape(1, D_MODEL), w_bf, cos, sin)


def _subln_out(o, g_ref):
    return _rms(o, g_ref[...]) * (1.0 - LAM_INIT)


def _attn_kernel(pt_ref, lq1, lk1, lq2, lk2, q_ref, k_ref, v_ref, g_ref,
                 dq_ref, dkn_ref, dvn_ref, ck_hbm, cv_hbm,
                 o_ref, do_ref,
                 m_ref, l_ref, acc_ref,
                 kpg, vpg, sem, kbuf, vbuf, dm_ref, dl_ref, dacc_ref, *,
                 tq, n_chunks, n_tiles):
    h = pl.program_id(0)
    i = pl.program_id(1)
    n_i = pl.num_programs(1)
    n_units = dq_ref.shape[0] * n_chunks
    half = dq_ref.shape[2]
    slots = PAGE_SIZE * N_HEADS
    lam = _lam(lq1, lk1, lq2, lk2)

    def page_copies(u):
        b = u // n_chunks
        c = u % n_chunks
        buf = u % 2
        copies = []
        for pg in range(DEC_PAGES):
            page = pt_ref[b, c * DEC_PAGES + pg]
            copies.append(pltpu.make_async_copy(ck_hbm.at[page], kpg.at[buf, pg], sem.at[0, buf]))
            copies.append(pltpu.make_async_copy(cv_hbm.at[0, page], vpg.at[buf, pg],
                                                sem.at[1, buf]))
        return copies

    def start_unit(u):
        for cp in page_copies(u):
            cp.start()

    def wait_unit(u):
        for cp in page_copies(u):
            cp.wait()

    def attend(b, k_of_map, v, state, causal):
        m_old, l_old, a_old = state
        keys = v.shape[0]
        row = lax.broadcasted_iota(jnp.int32, (half, keys), 0)
        col = lax.broadcasted_iota(jnp.int32, (half, keys), 1)
        ok = row % N_HEADS == col % N_HEADS
        if causal:
            ok = jnp.logical_and(ok, col // N_HEADS <= row // N_HEADS)
        s = jnp.concatenate(
            [jnp.where(ok, lax.dot_general(dq_ref[b, m], k_of_map(m), _NT,
                                           preferred_element_type=F32) * ATTN_SCALE,
                       -jnp.inf) for m in range(2)], axis=0)
        m_new = jnp.maximum(m_old, jnp.max(s, axis=1, keepdims=True))
        alpha = jnp.exp(m_old - m_new)
        p = jnp.exp(s - m_new)
        l_new = alpha * l_old + jnp.sum(p, axis=1, keepdims=True)
        a_new = alpha * a_old + jnp.dot(p.astype(BF16), v, preferred_element_type=F32)
        return m_new, l_new, a_new

    def decode_unit(u):
        b = u // n_chunks
        first = u % n_chunks == 0
        buf = u % 2
        for pg in range(DEC_PAGES):
            dst = slice(pg * slots, (pg + 1) * slots)
            for m in range(2):
                kbuf[m, dst, :] = kpg[buf, pg, pl.ds(m, slots, stride=2), :].astype(BF16)
            vbuf[dst, :] = vpg[buf, pg].reshape(slots, 2 * HEAD_DIM).astype(BF16)
        state = (jnp.where(first, -jnp.inf, dm_ref[...]),
                 jnp.where(first, 0.0, dl_ref[...]),
                 jnp.where(first, 0.0, dacc_ref[...]))
        state = attend(b, lambda m: kbuf[m], vbuf[...], state, causal=False)
        dm_ref[...], dl_ref[...], dacc_ref[...] = state
        _, l_fin, a_fin = attend(b, lambda m: dkn_ref[b, m], dvn_ref[b], state, causal=True)
        o = a_fin * (1.0 / l_fin)
        do_ref[b] = _subln_out(o[:half] - lam * o[half:], g_ref).astype(BF16)

    groups_per_head = sum(t // KV_GROUP for t in range(n_tiles))
    n_hosted = N_HEADS * groups_per_head
    g_i = i // KV_GROUP
    u_base = (h * groups_per_head
              + (KV_GROUP * g_i * (g_i - 1)) // 2 + (i % KV_GROUP) * g_i)

    @pl.when(jnp.logical_and(h == 0, i == 0))
    def _():
        start_unit(0)

    m_ref[...] = jnp.full(m_ref.shape, -jnp.inf, F32)
    l_ref[...] = jnp.zeros(l_ref.shape, F32)
    acc_ref[...] = jnp.zeros(acc_ref.shape, F32)

    def block(j, masked):
        start = pl.multiple_of(j * tq, tq)
        kb = k_ref[pl.ds(start, tq), :]
        vb = v_ref[pl.ds(start, tq), :]
        for mi in range(2):
            cs = slice(mi * HEAD_DIM, (mi + 1) * HEAD_DIM)
            s = lax.dot_general(q_ref[:, cs], kb[:, cs], _NT,
                                preferred_element_type=F32)
            if masked:
                row = lax.broadcasted_iota(jnp.int32, s.shape, 0)
                col = lax.broadcasted_iota(jnp.int32, s.shape, 1)
                s = jnp.where(col - row <= (i - j) * tq, s, -jnp.inf)
            m_old = m_ref[mi]
            m_new = jnp.maximum(m_old, jnp.max(s, axis=1, keepdims=True))
            alpha = jnp.exp2((m_old - m_new) * EXP2_SCALE)
            p = jnp.exp2((s - _lanes(m_new, tq)) * EXP2_SCALE)
            l_ref[mi] = alpha * l_ref[mi] + jnp.sum(p, axis=1, keepdims=True)
            acc_ref[mi] = _lanes(alpha, 2 * HEAD_DIM) * acc_ref[mi] + jnp.dot(
                p.astype(BF16), vb, preferred_element_type=F32)
            m_ref[mi] = m_new

    def group(g, carry):
        u = u_base + g
        wait_unit(u)
        start_unit(u + 1)
        decode_unit(u)
        for blk in range(KV_GROUP):
            block(g * KV_GROUP + blk, False)
        return carry

    def single(j, carry):
        block(j, True)
        return carry

    n_full = i // KV_GROUP
    lax.fori_loop(0, n_full, group, 0)
    lax.fori_loop(n_full * KV_GROUP, i + 1, single, 0)

    inv1 = _lanes(1.0 / l_ref[0], 2 * HEAD_DIM)
    inv2 = _lanes(1.0 / l_ref[1], 2 * HEAD_DIM)
    o = acc_ref[0] * inv1 - lam * (acc_ref[1] * inv2)
    o_ref[...] = _subln_out(o, g_ref).astype(BF16)

    @pl.when(jnp.logical_and(h == N_HEADS - 1, i == n_i - 1))
    def _():
        def drain(u, carry):
            wait_unit(u)

            @pl.when(u + 1 < n_units)
            def _():
                start_unit(u + 1)

            decode_unit(u)
            return carry

        lax.fori_loop(n_hosted, n_units, drain, 0)


def _attn(page_table, lams, q, k, v, g, dq, dkn, dvn, cache_k, cache_v, tq):
    s = q.shape[0]
    nb, _, half, _ = dq.shape
    n_tiles = s // tq
    n_chunks = page_table.shape[1] // DEC_PAGES
    assert page_table.shape[1] % DEC_PAGES == 0
    assert N_HEADS * sum(t // KV_GROUP for t in range(n_tiles)) < nb * n_chunks
    vw = 2 * HEAD_DIM
    slots = PAGE_SIZE * N_HEADS
    vec = pl.BlockSpec((1, HEAD_DIM), lambda h, i, pt: (0, 0))
    whole = lambda a: pl.BlockSpec(a.shape, lambda h, i, pt: (0,) * a.ndim)
    resident = lambda: pl.BlockSpec((s, vw), lambda h, i, pt: (0, h),
                                    pipeline_mode=pl.Buffered(1))
    grid_spec = pltpu.PrefetchScalarGridSpec(
        num_scalar_prefetch=1,
        grid=(N_HEADS, n_tiles),
        in_specs=[vec, vec, vec, vec,
                  pl.BlockSpec((tq, vw), lambda h, i, pt: (i, h)),
                  resident(), resident(),
                  pl.BlockSpec((1, vw), lambda h, i, pt: (0, 0)),
                  whole(dq), whole(dkn), whole(dvn),
                  pl.BlockSpec(memory_space=pl.ANY),
                  pl.BlockSpec(memory_space=pl.ANY)],
        out_specs=[pl.BlockSpec((tq, vw), lambda h, i, pt: (i, h)),
                   pl.BlockSpec((nb, half, vw), lambda h, i, pt: (0, 0, 0))],
        scratch_shapes=[pltpu.VMEM((2, tq, LANES), F32),
                        pltpu.VMEM((2, tq, LANES), F32),
                        pltpu.VMEM((2, tq, vw), F32),
                        pltpu.VMEM((2, DEC_PAGES, 2 * slots, HEAD_DIM), F32),
                        pltpu.VMEM((2, DEC_PAGES, PAGE_SIZE, N_HEADS, vw), F32),
                        pltpu.SemaphoreType.DMA((2, 2)),
                        pltpu.VMEM((2, DEC_PAGES * slots, HEAD_DIM), BF16),
                        pltpu.VMEM((DEC_PAGES * slots, vw), BF16),
                        pltpu.VMEM((2 * half, 1), F32),
                        pltpu.VMEM((2 * half, 1), F32),
                        pltpu.VMEM((2 * half, vw), F32)])
    return pl.pallas_call(
        functools.partial(_attn_kernel, tq=tq, n_chunks=n_chunks, n_tiles=n_tiles),
        grid_spec=grid_spec,
        out_shape=[jax.ShapeDtypeStruct((s, DIFF_WIDTH), BF16),
                   jax.ShapeDtypeStruct((nb, half, vw), BF16)],
        compiler_params=_params("arbitrary", "arbitrary"),
        name="diff_attn",
    )(page_table, *lams, q, k, v, g.reshape(1, vw), dq, dkn, dvn, cache_k, cache_v)


def _mem_attn_kernel(q_ref, mk_ref, mv_ref, o_ref):
    scale = (MEM_WIDTH // MEM_HEADS) ** -0.5
    for b in range(q_ref.shape[0]):
        for h in range(MEM_HEADS):
            cs = slice(h * HEAD_DIM, (h + 1) * HEAD_DIM)
            kh = mk_ref[b, :, h, :].astype(BF16)
            vh = mv_ref[b, :, h, :].astype(BF16)
            s = lax.dot_general(q_ref[b, :, cs], kh, _NT,
                                preferred_element_type=F32) * scale
            e = jnp.exp(s - jnp.max(s, axis=1, keepdims=True))
            p = e / jnp.sum(e, axis=1, keepdims=True)
            o_ref[b, :, cs] = jnp.dot(p.astype(BF16), vh,
                                      preferred_element_type=F32).astype(BF16)


def _mem_attn(q, mk, mv, tq, nbs):
    nb, t, _ = q.shape
    mem_spec = pl.BlockSpec((nbs, MEM_LEN, MEM_HEADS, HEAD_DIM), lambda b, i: (b, 0, 0, 0))
    return pl.pallas_call(
        _mem_attn_kernel,
        grid=(nb // nbs, t // tq),
        in_specs=[pl.BlockSpec((nbs, tq, MEM_WIDTH), lambda b, i: (b, i, 0)),
                  mem_spec, mem_spec],
        out_specs=pl.BlockSpec((nbs, tq, MEM_WIDTH), lambda b, i: (b, i, 0)),
        out_shape=jax.ShapeDtypeStruct((nb, t, MEM_WIDTH), BF16),
        compiler_params=_params("arbitrary", "arbitrary"),
        name="mem_xattn",
    )(q, mk, mv)


def _mix_kernel(oatt_ref, omem_ref, p_ref, x_ref, hist_ref, wp_ref, ps_ref, wo_ref,
                h_ref, pext_ref, cat_ref, *, tm, bq, pos0, n_i):
    i = pl.program_id(0)
    hr = POOL_HIST_PAD * bq
    if n_i > 1:
        @pl.when(i == 0)
        def _():
            pext_ref[0:hr, :] = hist_ref[...]
    else:
        pext_ref[0:hr, :] = hist_ref[...]
    p = p_ref[...]
    pext_ref[hr:hr + tm, :] = p
    rows = i * tm + lax.broadcasted_iota(jnp.int32, (tm, 1), 0)
    pos = pos0 + rows // bq
    for g, w in enumerate(POOL_WINDOWS):
        cs = slice(g * POOL_GROUP_DIM, (g + 1) * POOL_GROUP_DIM)
        acc = p[:, cs]
        for k in range(1, w):
            acc = acc + pext_ref[hr - k * bq:hr - k * bq + tm, cs]
        cnt = jnp.minimum(pos + 1, w).astype(F32)
        d = acc / cnt - p[:, cs]
        y = jnp.dot(d.astype(BF16), wp_ref[g], preferred_element_type=F32) * ps_ref[:, cs]
        cat_ref[:, DIFF_WIDTH + g * POOL_GROUP_DIM:DIFF_WIDTH + (g + 1) * POOL_GROUP_DIM] = (
            y.astype(BF16))
    cat_ref[:, 0:DIFF_WIDTH] = oatt_ref[...]
    cat_ref[:, DIFF_WIDTH + POOL_WIDTH:] = omem_ref[...]
    h_ref[...] = x_ref[...] + jnp.dot(cat_ref[...], wo_ref[...], preferred_element_type=F32)
    if n_i > 1:
        pext_ref[0:hr, :] = pext_ref[tm:tm + hr, :]


def _mix(oatt, omem, p, x, hist, wp_bf, pool_scale, wo_bf, tm, bq, pos0):
    r = x.shape[0]
    n_i = r // tm
    hr = POOL_HIST_PAD * bq
    row = lambda i: (i, 0)
    fixed = lambda i: (0, 0)
    return pl.pallas_call(
        functools.partial(_mix_kernel, tm=tm, bq=bq, pos0=pos0, n_i=n_i),
        grid=(n_i,),
        in_specs=[pl.BlockSpec((tm, DIFF_WIDTH), row),
                  pl.BlockSpec((tm, MEM_WIDTH), row),
                  pl.BlockSpec((tm, POOL_WIDTH), row),
                  pl.BlockSpec((tm, D_MODEL), row),
                  pl.BlockSpec((hr, POOL_WIDTH), fixed),
                  pl.BlockSpec((len(POOL_WINDOWS), POOL_GROUP_DIM, POOL_GROUP_DIM),
                               lambda i: (0, 0, 0)),
                  pl.BlockSpec((1, POOL_WIDTH), fixed),
                  pl.BlockSpec((D_MODEL, D_MODEL), fixed)],
        out_specs=pl.BlockSpec((tm, D_MODEL), row),
        out_shape=jax.ShapeDtypeStruct((r, D_MODEL), F32),
        scratch_shapes=[pltpu.VMEM((hr + tm, POOL_WIDTH), F32),
                        pltpu.VMEM((tm, D_MODEL), BF16)],
        compiler_params=_params("arbitrary"),
        name="pool_mix_out_proj",
    )(oatt, omem, p, x, hist, wp_bf, pool_scale.reshape(1, POOL_WIDTH), wo_bf)


def _ffn_kernel(h_ref, g_ref, wg_ref, wv_ref, cwg_ref, cwv_ref, cbg_ref, cbv_ref,
                hg_ref, hv_ref, wd_ref, fn_ref,
                y_ref, tg_ref, tv_ref,
                hn_ref, ug_ref, uv_ref, cg_ref, cv_ref, *, tm, bq, head, n_i):
    i = pl.program_id(0)
    j = pl.program_id(1)

    @pl.when(j == 0)
    def _():
        h = h_ref[...]
        hn_ref[...] = _rms(h, g_ref[...]).astype(BF16)
        y_ref[...] = h

    if n_i > 1:
        @pl.when(i == 0)
        def _():
            cg_ref[j] = hg_ref[...]
            cv_ref[j] = hv_ref[...]

    hn = hn_ref[...]

    def up_proj(w_ref, hist_ref, u_ref, carry_ref, tail_ref):
        u = jnp.dot(hn, w_ref[...], preferred_element_type=F32)
        u_ref[head:head + tm, :] = u
        last = u[tm - head:, :]
        if n_i > 1:
            u_ref[0:head, :] = carry_ref[j]
            carry_ref[j] = last
        else:
            u_ref[0:head, :] = hist_ref[...]
        tail_ref[0] = last

    def conv(cw_ref, cb_ref, u_ref, cs):
        c = cb_ref[:, cs] + cw_ref[0:1, cs] * u_ref[head - 2 * bq:head - 2 * bq + tm, cs]
        c = c + cw_ref[1:2, cs] * u_ref[head - bq:head - bq + tm, cs]
        return c + cw_ref[2:3, cs] * u_ref[head:head + tm, cs]

    up_proj(wg_ref, hg_ref, ug_ref, cg_ref, tg_ref)
    up_proj(wv_ref, hv_ref, uv_ref, cv_ref, tv_ref)
    tf = ug_ref.shape[1]
    down = None
    for s in range(tf // FFN_SLAB):
        cs = slice(s * FFN_SLAB, (s + 1) * FFN_SLAB)
        gate = conv(cwg_ref, cbg_ref, ug_ref, cs)
        val = conv(cwv_ref, cbv_ref, uv_ref, cs)
        act = gate * (1.0 / (1.0 + jnp.exp(-gate))) * val
        part = jnp.dot(act.astype(BF16), wd_ref[cs, :], preferred_element_type=F32)
        down = part if down is None else down + part
    y_ref[...] += down

    @pl.when(j == pl.num_programs(1) - 1)
    def _():
        y_ref[...] = _rms(y_ref[...], fn_ref[...])


def _ffn(h, g, wup_bf, conv_w, conv_b, hist, wd_bf, fn, tm, tf, bq):
    r = h.shape[0]
    n_i = r // tm
    n_j = D_FF // tf
    head = hist.shape[0]
    assert head % SUBLANES == 0 and head >= 2 * bq
    assert n_i == 1 or head == SUBLANES
    row = lambda i, j: (i, 0)
    fixed = lambda i, j: (0, 0)
    gcol = lambda i, j: (0, j)
    vcol = lambda i, j: (0, n_j + j)
    conv_b2 = conv_b.reshape(1, 2 * D_FF)
    y, tg, tv = pl.pallas_call(
        functools.partial(_ffn_kernel, tm=tm, bq=bq, head=head, n_i=n_i),
        grid=(n_i, n_j),
        in_specs=[pl.BlockSpec((tm, D_MODEL), row),
                  pl.BlockSpec((1, D_MODEL), fixed),
                  pl.BlockSpec((D_MODEL, tf), gcol),
                  pl.BlockSpec((D_MODEL, tf), vcol),
                  pl.BlockSpec((CONV_W, tf), gcol),
                  pl.BlockSpec((CONV_W, tf), vcol),
                  pl.BlockSpec((1, tf), gcol),
                  pl.BlockSpec((1, tf), vcol),
                  pl.BlockSpec((head, tf), gcol),
                  pl.BlockSpec((head, tf), vcol),
                  pl.BlockSpec((tf, D_MODEL), lambda i, j: (j, 0)),
                  pl.BlockSpec((1, D_MODEL), fixed)],
        out_specs=[pl.BlockSpec((tm, D_MODEL), row),
                   pl.BlockSpec((1, head, tf), lambda i, j: (i, 0, j)),
                   pl.BlockSpec((1, head, tf), lambda i, j: (i, 0, j))],
        out_shape=[jax.ShapeDtypeStruct((r, D_MODEL), F32),
                   jax.ShapeDtypeStruct((n_i, head, D_FF), F32),
                   jax.ShapeDtypeStruct((n_i, head, D_FF), F32)],
        scratch_shapes=[pltpu.VMEM((tm, D_MODEL), BF16),
                        pltpu.VMEM((head + tm, tf), F32),
                        pltpu.VMEM((head + tm, tf), F32),
                        pltpu.VMEM((n_j, head, tf), F32),
                        pltpu.VMEM((n_j, head, tf), F32)],
        compiler_params=_params("arbitrary", "arbitrary"),
        name="conv_gated_ffn",
    )(h, g.reshape(1, D_MODEL), wup_bf, wup_bf, conv_w, conv_w, conv_b2, conv_b2,
      hist, hist, wd_bf, fn.reshape(1, D_MODEL))
    return y, jnp.concatenate([tg[n_i - 1], tv[n_i - 1]], axis=1)


def _rope_tables(pos):
    half = HEAD_DIM // 2
    inv = ROPE_THETA ** (-jnp.arange(half, dtype=F32) / half)
    ang = pos.astype(F32)[:, None] * inv[None, :]
    cos, sin = jnp.cos(ang), jnp.sin(ang)
    return jnp.concatenate([cos, cos], axis=1), jnp.concatenate([-sin, sin], axis=1)


def kernel(x_prompt, x_sample, cache_k, cache_v, page_table, state_pool, state_conv,
           cache_mem_k, cache_mem_v, mem_prompt, attn_norm, w_in, lam_q1, lam_k1,
           lam_q2, lam_k2, subln_norm, w_pool, pool_scale, mem_norm, w_mem_kv, w_o,
           ffn_norm, w_up, conv_w, conv_b, w_down, final_norm):
    assert w_in.shape[0] == 1 and x_prompt.shape[0] == 1
    _, seq, _ = x_prompt.shape
    nb, n_new, _ = x_sample.shape
    past = page_table.shape[1] * cache_k.shape[2]
    n_phys = cache_k.shape[1]

    w_in_b = w_in[0].astype(BF16)
    w_mem_b = w_mem_kv[0].astype(BF16)
    w_o_b = w_o[0].astype(BF16)
    w_up_b = w_up[0].astype(BF16)
    w_down_b = w_down[0].astype(BF16)
    w_pool_b = w_pool[0].astype(BF16)
    lams = [v[0].reshape(1, HEAD_DIM) for v in (lam_q1, lam_k1, lam_q2, lam_k2)]

    xp = x_prompt.reshape(seq, D_MODEL)
    cos_p, sin_p = _rope_tables(jnp.arange(seq, dtype=jnp.int32))
    q_p, kf_p, kb_p, vf_p, vb_p, p_p, cq_p = _in_proj(
        xp, attn_norm[0], w_in_b, cos_p, sin_p, tm=512)
    zmem = _rms_matmul(mem_prompt[0], mem_norm[0], w_mem_b, tn=MEM_WIDTH)
    mem_dims = (MEM_LEN, MEM_HEADS, HEAD_DIM)
    mk_p = zmem[:, :MEM_WIDTH].reshape(1, *mem_dims)
    mv_p = zmem[:, MEM_WIDTH:].reshape(1, *mem_dims)

    rows = nb * n_new
    xs = x_sample.reshape(rows, D_MODEL)
    pos_s = past + jnp.arange(rows, dtype=jnp.int32) % n_new
    cos_s, sin_s = _rope_tables(pos_s)
    q_s, kf_s, kb_s, vf_s, vb_s, p_s, cq_s = _in_proj(
        xs, attn_norm[0], w_in_b, cos_s, sin_s, tm=rows)

    def by_map(a):
        a = a.reshape(nb, n_new, N_HEADS, 2, HEAD_DIM)
        return jnp.transpose(a, (0, 3, 1, 2, 4)).reshape(nb, 2, n_new * N_HEADS, HEAD_DIM)

    oatt_p, oatt_s = _attn(
        page_table, lams, q_p, kb_p, vb_p, subln_norm[0],
        by_map(q_s), by_map(kb_s), vb_s.reshape(nb, n_new * N_HEADS, 2 * HEAD_DIM),
        cache_k.reshape(n_phys, PAGE_SIZE * N_HEADS * 2, HEAD_DIM), cache_v, tq=512)
    oatt_s = oatt_s.reshape(nb, n_new, DIFF_WIDTH)

    omem_p = _mem_attn(cq_p[None], mk_p, mv_p, tq=512, nbs=1)[0]
    h_p = _mix(oatt_p, omem_p, p_p, xp, jnp.zeros((POOL_HIST_PAD, POOL_WIDTH), F32),
               w_pool_b, pool_scale[0], w_o_b, tm=512, bq=1, pos0=0)
    y_p, tail_p = _ffn(h_p, ffn_norm[0], w_up_b, conv_w[0], conv_b[0],
                       jnp.zeros((SUBLANES, 2 * D_FF), F32), w_down_b, final_norm,
                       tm=512, tf=512, bq=1)

    cq_pad = jnp.pad(cq_s.reshape(nb, n_new, MEM_WIDTH),
                     ((0, 0), (0, SUBLANES - n_new), (0, 0)))
    omem_s = _mem_attn(cq_pad, cache_mem_k[0], cache_mem_v[0],
                       tq=SUBLANES, nbs=SUBLANES)[:, :n_new]

    def tmaj(a):
        return jnp.swapaxes(a, 0, 1).reshape(a.shape[0] * a.shape[1], a.shape[2])

    def smaj(a, t):
        return jnp.swapaxes(a.reshape(t, nb, a.shape[1]), 0, 1)

    hist_pool = jnp.pad(tmaj(state_pool[0]), ((nb, 0), (0, 0)))
    h_s = _mix(tmaj(oatt_s), tmaj(omem_s), tmaj(p_s.reshape(nb, n_new, POOL_WIDTH)),
               tmaj(x_sample), hist_pool, w_pool_b, pool_scale[0], w_o_b,
               tm=rows, bq=nb, pos0=past)
    y_s, tail_s = _ffn(h_s, ffn_norm[0], w_up_b, conv_w[0], conv_b[0],
                       tmaj(state_conv[0]), w_down_b, final_norm,
                       tm=rows, tf=512, bq=nb)

    pool_s = jnp.concatenate([state_pool[0], p_s.reshape(nb, n_new, POOL_WIDTH)],
                             axis=1)[:, -POOL_HIST:]
    return (
        y_p.reshape(1, seq, D_MODEL),
        smaj(y_s, n_new),
        kf_p,
        vf_p,
        kf_s.reshape(1, nb, n_new, N_HEADS, 2, HEAD_DIM),
        vf_s.reshape(1, nb, n_new, N_HEADS, 2 * HEAD_DIM),
        p_p[seq - POOL_HIST:].reshape(1, 1, POOL_HIST, POOL_WIDTH),
        pool_s[None],
        tail_p[SUBLANES - (CONV_W - 1):].reshape(1, 1, CONV_W - 1, 2 * D_FF),
        smaj(tail_s, CONV_W - 1)[None],
        mk_p[None],
        mv_p[None],
    )
```
